```python
import jax, jax.numpy as jnp
from jax import lax
import numpy as np

D_MODEL = 1024
BATCH = 8
SEQ = 4096
DEPTH = 4

N_MIXERS = 2
CONV_KERNEL = 31
N_HEADS = 16
N_KV_HEADS = 4
HEAD_DIM = D_MODEL // N_HEADS
ROT_DIM = HEAD_DIM // 4
ROPE_THETA = 500000.0
WINDOW = 128
BLOCK = 128
D_FF_DENSE = 256 * ((8 * D_MODEL // 3 + 255) // 256)
N_EXPERTS = 8
TOP_K = 2
D_FF_EXPERT = 7 * D_MODEL // 2
EPS = 1e-6
N_CONV_LAYERS = (DEPTH + 1) // 2
N_ATTN_LAYERS = DEPTH // 2

kernel_name = "hybrid_conformer_swa_sink_moe_adaln"


def rms_norm(x, g):
    xf = x.astype(jnp.float32)
    y = xf * lax.rsqrt(jnp.mean(xf * xf, axis=-1, keepdims=True) + EPS)
    return (y * g.astype(jnp.float32)).astype(x.dtype)


def layer_norm(x, g, b):
    xf = x.astype(jnp.float32)
    mu = jnp.mean(xf, axis=-1, keepdims=True)
    var = jnp.mean(jnp.square(xf - mu), axis=-1, keepdims=True)
    y = (xf - mu) * lax.rsqrt(var + EPS)
    return (y * g.astype(jnp.float32) + b.astype(jnp.float32)).astype(x.dtype)


def modulate(h, shift, scale):
    return h * (1 + scale[:, None, :]) + shift[:, None, :]


def conformer_conv(h, w_pw1, b_pw1, w_dw, b_dw, ln_g, ln_b, w_pw2, b_pw2):
    u = h @ w_pw1 + b_pw1
    a, g = jnp.split(u, 2, axis=-1)
    u = a * jax.nn.sigmoid(g)
    u = lax.conv_general_dilated(
        u, w_dw[:, None, :].astype(u.dtype), window_strides=(1,),
        padding=[(CONV_KERNEL - 1, 0)],
        dimension_numbers=('NWC', 'WIO', 'NWC'),
        feature_group_count=D_MODEL) + b_dw
    u = jax.nn.silu(layer_norm(u, ln_g, ln_b))
    return u @ w_pw2 + b_pw2


def rope_tables(positions):
    inv_freq = ROPE_THETA ** (-jnp.arange(0, ROT_DIM, 2, dtype=jnp.float32) / ROT_DIM)
    ang = positions.astype(jnp.float32)[..., None] * inv_freq
    return jnp.cos(ang), jnp.sin(ang)


def apply_partial_rope(x, cos, sin):
    xr = x[..., :ROT_DIM].astype(jnp.float32)
    x1, x2 = jnp.split(xr, 2, axis=-1)
    c = cos[:, :, None, :]
    s = sin[:, :, None, :]
    rot = jnp.concatenate([x1 * c - x2 * s, x2 * c + x1 * s], axis=-1)
    return jnp.concatenate([rot.astype(x.dtype), x[..., ROT_DIM:]], axis=-1)


def swa_sink_attention(h, cos, sin, w_qkv, q_gain, k_gain, sinks, w_o):
    B, S, _ = h.shape
    nb = S // BLOCK
    G = N_HEADS // N_KV_HEADS
    qkv = h @ w_qkv
    nq = N_HEADS * HEAD_DIM
    nk = N_KV_HEADS * HEAD_DIM
    q = qkv[..., :nq].reshape(B, S, N_HEADS, HEAD_DIM)
    k = qkv[..., nq:nq + nk].reshape(B, S, N_KV_HEADS, HEAD_DIM)
    v = qkv[..., nq + nk:].reshape(B, S, N_KV_HEADS, HEAD_DIM)
    q = apply_partial_rope(rms_norm(q, q_gain), cos, sin)
    k = apply_partial_rope(rms_norm(k, k_gain), cos, sin)
    q = q.reshape(B, nb, BLOCK, N_KV_HEADS, G, HEAD_DIM)

    def band(t):
        prev = jnp.pad(t, ((0, 0), (BLOCK, 0), (0, 0), (0, 0)))[:, :S]
        prev = prev.reshape(B, nb, BLOCK, N_KV_HEADS, HEAD_DIM)
        cur = t.reshape(B, nb, BLOCK, N_KV_HEADS, HEAD_DIM)
        return jnp.concatenate([prev, cur], axis=2)

    kb, vb = band(k), band(v)
    s = jnp.einsum('bnqkgd,bnjkd->bnkgqj', q, kb).astype(jnp.float32) * (HEAD_DIM ** -0.5)
    qi = jnp.arange(BLOCK)[:, None]
    kj = jnp.arange(2 * BLOCK)[None, :]
    diff = qi + BLOCK - kj
    in_win = (diff >= 0) & (diff < WINDOW)
    key_pos = jnp.arange(nb)[:, None] * BLOCK - BLOCK + kj
    valid = in_win[None] & (key_pos[:, None, :] >= 0)
    s = jnp.where(valid[None, :, None, None], s, jnp.float32(-1e30))
    sink = sinks.astype(jnp.float32).reshape(N_KV_HEADS, G)[None, None, :, :, None, None]
    m = jnp.maximum(jnp.max(s, axis=-1, keepdims=True), sink)
    p = jnp.exp(s - m)
    p = p / (jnp.sum(p, axis=-1, keepdims=True) + jnp.exp(sink - m))
    o = jnp.einsum('bnkgqj,bnjkd->bnqkgd', p.astype(vb.dtype), vb)
    return o.reshape(B, S, N_HEADS * HEAD_DIM) @ w_o


def swiglu(h, w_gate, w_up, w_down):
    return (jax.nn.silu(h @ w_gate) * (h @ w_up)) @ w_down


def moe_swiglu(h, w_router, b_router, w_gate, w_up, w_down):
    B, S, D = h.shape
    t = h.reshape(B * S, D)
    logits = (t @ w_router).astype(jnp.float32) + b_router.astype(jnp.float32)
    vals, idx = lax.top_k(logits, TOP_K)
    gates = jax.nn.softmax(vals, axis=-1)
    combine = jnp.sum(jax.nn.one_hot(idx, N_EXPERTS, dtype=jnp.float32) * gates[..., None], axis=1)
    out = jnp.zeros_like(t)
    for e in range(N_EXPERTS):
        out = out + combine[:, e:e + 1].astype(t.dtype) * swiglu(t, w_gate[e], w_up[e], w_down[e])
    return out.reshape(B, S, D)


def setup_inputs(seed: int = 0) -> dict:
    key = jax.random.key(seed)
    ks = iter(jax.random.split(key, 40))
    D = D_MODEL
    f32 = jnp.float32

    def nrm(shape, scale):
        return jax.random.normal(next(ks), shape, f32) * scale

    nc, na = N_CONV_LAYERS, N_ATTN_LAYERS
    qkv_out = (N_HEADS + 2 * N_KV_HEADS) * HEAD_DIM
    offsets = jax.random.randint(next(ks), (BATCH, 1), 0, 1024, dtype=jnp.int32)
    positions = offsets + jnp.arange(SEQ, dtype=jnp.int32)[None, :]
    return {
        "x": nrm((BATCH, SEQ, D), 1.0),
        "c": nrm((BATCH, D), 1.0),
        "positions": positions,
        "norm_g": 1.0 + nrm((DEPTH, 2, D), 0.02),
        "w_ada": nrm((DEPTH, D, 6 * D), 0.5 * D ** -0.5),
        "b_ada": nrm((DEPTH, 6 * D), 0.02),
        "conv_w_pw1": nrm((nc, D, 2 * D), D ** -0.5),
        "conv_b_pw1": nrm((nc, 2 * D), 0.02),
        "conv_w_dw": nrm((nc, CONV_KERNEL, D), CONV_KERNEL ** -0.5),
        "conv_b_dw": nrm((nc, D), 0.02),
        "conv_ln_g": 1.0 + nrm((nc, D), 0.02),
        "conv_ln_b": nrm((nc, D), 0.02),
        "conv_w_pw2": nrm((nc, D, D), D ** -0.5),
        "conv_b_pw2": nrm((nc, D), 0.02),
        "attn_w_qkv": nrm((na, D, qkv_out), D ** -0.5),
        "attn_q_gain": 1.0 + nrm((na, HEAD_DIM), 0.02),
        "attn_k_gain": 1.0 + nrm((na, HEAD_DIM), 0.02),
        "attn_sinks": nrm((na, N_HEADS), 1.0),
        "attn_w_o": nrm((na, N_HEADS * HEAD_DIM, D), (N_HEADS * HEAD_DIM) ** -0.5),
        "ffn_w_gate": nrm((nc, D, D_FF_DENSE), D ** -0.5),
        "ffn_w_up": nrm((nc, D, D_FF_DENSE), D ** -0.5),
        "ffn_w_down": nrm((nc, D_FF_DENSE, D), D_FF_DENSE ** -0.5),
        "moe_w_router": nrm((na, D, N_EXPERTS), D ** -0.5),
        "moe_b_router": nrm((na, N_EXPERTS), 0.01),
        "moe_w_gate": nrm((na, N_EXPERTS, D, D_FF_EXPERT), D ** -0.5),
        "moe_w_up": nrm((na, N_EXPERTS, D, D_FF_EXPERT), D ** -0.5),
        "moe_w_down": nrm((na, N_EXPERTS, D_FF_EXPERT, D), D_FF_EXPERT ** -0.5),
    }


def reference(x, c, positions, norm_g, w_ada, b_ada,
              conv_w_pw1, conv_b_pw1, conv_w_dw, conv_b_dw, conv_ln_g, conv_ln_b, conv_w_pw2, conv_b_pw2,
              attn_w_qkv, attn_q_gain, attn_k_gain, attn_sinks, attn_w_o,
              ffn_w_gate, ffn_w_up, ffn_w_down,
              moe_w_router, moe_b_router, moe_w_gate, moe_w_up, moe_w_down):
    cos, sin = rope_tables(positions)
    c_act = jax.nn.silu(c)
    for i in range(DEPTH):
        j = i // 2
        mod = c_act @ w_ada[i] + b_ada[i]
        sh1, sc1, g1, sh2, sc2, g2 = jnp.split(mod, 6, axis=-1)
        h = modulate(rms_norm(x, norm_g[i, 0]), sh1, sc1)
        if i % N_MIXERS == 0:
            y = conformer_conv(h, conv_w_pw1[j], conv_b_pw1[j], conv_w_dw[j], conv_b_dw[j],
                               conv_ln_g[j], conv_ln_b[j], conv_w_pw2[j], conv_b_pw2[j])
        else:
            y = swa_sink_attention(h, cos, sin, attn_w_qkv[j], attn_q_gain[j], attn_k_gain[j],
                                   attn_sinks[j], attn_w_o[j])
        x = x + g1[:, None, :] * y
        h = modulate(rms_norm(x, norm_g[i, 1]), sh2, sc2)
        if i % 2 == 0:
            y = swiglu(h, ffn_w_gate[j], ffn_w_up[j], ffn_w_down[j])
        else:
            y = moe_swiglu(h, moe_w_router[j], moe_b_router[j], moe_w_gate[j], moe_w_up[j], moe_w_down[j])
        x = x + g2[:, None, :] * y
    return x
```

```python
import functools
from typing import NamedTuple

import jax
import jax.numpy as jnp
from jax import lax
from jax.experimental import pallas as pl
from jax.experimental.pallas import tpu as pltpu

F32 = jnp.float32
BF16 = jnp.bfloat16
I32 = jnp.int32

V7X_LANES = 128
V7X_SUBLANES = 8
V7X_MXU_DIM = 256
V7X_VMEM_LIMIT_BYTES = 56 * 1024 * 1024

ROPE_THETA = 500000.0
EPS = 1e-6
ATTN_BLOCK = 128
TOP_K = 2
MASK_VALUE = -1e30


class Dims(NamedTuple):
    batch: int
    seq: int
    d_model: int
    depth: int
    conv_kernel: int
    n_heads: int
    n_kv_heads: int
    head_dim: int
    d_ff_dense: int
    n_experts: int
    d_ff_expert: int


class Plan(NamedTuple):
    tm: int
    tf_dense: int
    tf_expert: int
    tg: int
    tn_ada: int
    conv_rows: int
    conv_cols: int


def _largest_tile(total, target, quantum):
    best = None
    t = quantum
    while t <= min(total, target):
        if total % t == 0:
            best = t
        t += quantum
    return best if best is not None else total


def _make_plan(d: Dims) -> Plan:
    tm = _largest_tile(d.seq, 512, ATTN_BLOCK)
    return Plan(
        tm=tm,
        tf_dense=_largest_tile(d.d_ff_dense, 1408, V7X_LANES),
        tf_expert=_largest_tile(d.d_ff_expert, 512, V7X_LANES),
        tg=_largest_tile(d.batch * d.seq, 1024, ATTN_BLOCK),
        tn_ada=_largest_tile(6 * d.d_model, 1536, V7X_LANES),
        conv_rows=32,
        conv_cols=min(d.d_model, 2 * V7X_LANES),
    )


def _params(*semantics):
    return pltpu.CompilerParams(dimension_semantics=semantics,
                                vmem_limit_bytes=V7X_VMEM_LIMIT_BYTES)


def _dot(a, b):
    return jnp.dot(a, b, preferred_element_type=F32)


def _sigmoid(x):
    return 1.0 / (1.0 + jnp.exp(-x))


def _norm_mod(x, g_row, shift_row, scale_row):
    y = x * lax.rsqrt(jnp.mean(x * x, axis=-1, keepdims=True) + EPS)
    return (y * g_row) * (1.0 + scale_row) + shift_row


def _full(shape):
    return pl.BlockSpec(shape, lambda *_: (0,) * len(shape))


def _ada_body(c_ref, w_ref, b_ref, o_ref):
    c = c_ref[...]
    ca = c * _sigmoid(c)
    o_ref[0] = jnp.dot(ca, w_ref[0], preferred_element_type=F32,
                       precision=lax.Precision.HIGHEST) + b_ref[0]


def _ada_all_layers(c, w_ada, b_ada, plan):
    depth, dm, n6 = w_ada.shape
    bsz = c.shape[0]
    tn = plan.tn_ada
    return pl.pallas_call(
        _ada_body,
        grid=(depth, n6 // tn),
        in_specs=[_full((bsz, dm)),
                  pl.BlockSpec((1, dm, tn), lambda i, j: (i, 0, j)),
                  pl.BlockSpec((1, 1, tn), lambda i, j: (i, 0, j))],
        out_specs=pl.BlockSpec((1, bsz, tn), lambda i, j: (i, 0, j)),
        out_shape=jax.ShapeDtypeStruct((depth, bsz, n6), F32),
        compiler_params=_params("arbitrary", "arbitrary"),
        name="adaln_mod",
    )(c, w_ada, b_ada.reshape(depth, 1, n6))


def _rope_body(pos_ref, f_ref, m1_ref, m2_ref, c_ref, s1_ref, s2_ref):
    ang = pos_ref[...].astype(F32) * f_ref[...]
    s = jnp.sin(ang)
    c_ref[...] = jnp.cos(ang)
    s1_ref[...] = s * m1_ref[...]
    s2_ref[...] = s * m2_ref[...]


def _rope_tables(positions, d: Dims, plan):
    t_total = d.batch * d.seq
    rot = d.head_dim // 4
    half = rot // 2
    inv_freq = ROPE_THETA ** (-jnp.arange(0, rot, 2, dtype=F32) / rot)
    lane = jnp.arange(V7X_LANES)
    in_head = lane % d.head_dim
    freq_row = jnp.where(in_head < rot, inv_freq[in_head % half], 0.0).astype(F32)[None]
    m1 = jnp.where((in_head >= half) & (in_head < rot), 1.0, 0.0).astype(F32)[None]
    m2 = jnp.where(in_head < half, -1.0, 0.0).astype(F32)[None]
    tm = plan.tm
    row = pl.BlockSpec((tm, V7X_LANES), lambda i: (i, 0))
    tab = jax.ShapeDtypeStruct((t_total, V7X_LANES), F32)
    return pl.pallas_call(
        _rope_body,
        grid=(t_total // tm,),
        in_specs=[pl.BlockSpec((tm, 1), lambda i: (i, 0)),
                  _full((1, V7X_LANES)), _full((1, V7X_LANES)), _full((1, V7X_LANES))],
        out_specs=[row, row, row],
        out_shape=[tab, tab, tab],
        compiler_params=_params("arbitrary"),
        name="rope_tables",
    )(positions.reshape(t_total, 1), freq_row, m1, m2)


def _conv_body(x_ref, mod_ref, ng_ref, w1_ref, b1_ref, wd_ref, bd_ref, lg_ref, lb_ref,
               w2_ref, b2_ref, o_ref, ubuf, vbuf, *, dm, ktaps, halo, tiles_per_seq, rc, cb):
    m = pl.program_id(0)
    tm = x_ref.shape[0]
    x = x_ref[...]
    mod = mod_ref[0]
    h = _norm_mod(x, ng_ref[...], mod[0:1], mod[1:2]).astype(BF16)
    u = _dot(h, w1_ref[...]) + b1_ref[...]
    glu = u[:, :dm] * _sigmoid(u[:, dm:])

    @pl.when(m % tiles_per_seq == 0)
    def _():
        ubuf[0:halo, :] = jnp.zeros((halo, dm), F32)

    ubuf[halo:halo + tm, :] = glu

    def chunk(ci, carry):
        r0 = pl.multiple_of(ci * rc, rc)
        pieces = []
        for cbi in range(dm // cb):
            cols = slice(cbi * cb, (cbi + 1) * cb)
            win = ubuf[pl.ds(r0, rc + halo), cols]
            acc = None
            for r in range(V7X_SUBLANES):
                z = None
                for q in range(halo // V7X_SUBLANES):
                    dl = V7X_SUBLANES * q + r
                    if dl >= ktaps:
                        continue
                    s = halo - V7X_SUBLANES * (q + 1)
                    term = wd_ref[dl:dl + 1, cols] * win[s:s + rc + V7X_SUBLANES, :]
                    z = term if z is None else z + term
                if z is None:
                    continue
                if r:
                    z = pltpu.roll(z, r, 0)
                z = z[V7X_SUBLANES:V7X_SUBLANES + rc, :]
                acc = z if acc is None else acc + z
            pieces.append(acc)
        conv = jnp.concatenate(pieces, axis=1) + bd_ref[...]
        mu = jnp.mean(conv, axis=-1, keepdims=True)
        cen = conv - mu
        var = jnp.mean(cen * cen, axis=-1, keepdims=True)
        y = cen * lax.rsqrt(var + EPS) * lg_ref[...] + lb_ref[...]
        vbuf[pl.ds(r0, rc), :] = (y * _sigmoid(y)).astype(BF16)
        return carry

    lax.fori_loop(0, tm // rc, chunk, 0)
    ubuf[0:halo, :] = ubuf[tm:tm + halo, :]
    y2 = _dot(vbuf[...], w2_ref[...]) + b2_ref[...]
    o_ref[...] = x + mod[2:3] * y2


def _conv_sublayer(x, mod, ng, w1, b1, w_dw, b_dw, ln_g, ln_b, w2, b2, d: Dims, plan):
    t_total, dm = x.shape
    tm = plan.tm
    ktaps = w_dw.shape[0]
    halo = -(-ktaps // V7X_SUBLANES) * V7X_SUBLANES
    wd = jnp.zeros((halo, dm), F32).at[:ktaps].set(w_dw[::-1])
    tiles_per_seq = d.seq // tm
    row = pl.BlockSpec((tm, dm), lambda i: (i, 0))
    body = functools.partial(_conv_body, dm=dm, ktaps=ktaps, halo=halo,
                             tiles_per_seq=tiles_per_seq, rc=plan.conv_rows, cb=plan.conv_cols)
    return pl.pallas_call(
        body,
        grid=(t_total // tm,),
        in_specs=[row,
                  pl.BlockSpec((1, 6, dm), lambda i: (i // tiles_per_seq, 0, 0)),
                  _full((1, dm)), _full((dm, 2 * dm)), _full((1, 2 * dm)),
                  _full(wd.shape), _full((1, dm)), _full((1, dm)), _full((1, dm)),
                  _full((dm, dm)), _full((1, dm))],
        out_specs=row,
        out_shape=jax.ShapeDtypeStruct((t_total, dm), F32),
        scratch_shapes=[pltpu.VMEM((halo + tm, dm), F32), pltpu.VMEM((tm, dm), BF16)],
        compiler_params=_params("arbitrary"),
        name="conv_sublayer",
    )(x, mod, ng[None], w1, b1[None], wd, b_dw[None], ln_g[None], ln_b[None], w2, b2[None])


def _swiglu_step(hbuf, wg, wu, wd, acc_ref, first):
    hh = hbuf[...]
    g = _dot(hh, wg)
    u = _dot(hh, wu)
    a = ((g * _sigmoid(g)) * u).astype(BF16)
    part = _dot(a, wd)

    @pl.when(first)
    def _():
        acc_ref[...] = part

    @pl.when(jnp.logical_not(first))
    def _():
        acc_ref[...] += part


def _ffn_body(x_ref, mod_ref, ng_ref, wg_ref, wu_ref, wd_ref, o_ref, hbuf, acc_ref):
    f = pl.program_id(1)
    mod = mod_ref[0]

    @pl.when(f == 0)
    def _():
        hbuf[...] = _norm_mod(x_ref[...], ng_ref[...], mod[3:4], mod[4:5]).astype(BF16)

    _swiglu_step(hbuf, wg_ref[...], wu_ref[...], wd_ref[...], acc_ref, f == 0)

    @pl.when(f == pl.num_programs(1) - 1)
    def _():
        o_ref[...] = x_ref[...] + mod[5:6] * acc_ref[...]


def _ffn_sublayer(x, mod, ng, wg, wu, wd, d: Dims, plan):
    t_total, dm = x.shape
    ff = wg.shape[1]
    tm, tf = plan.tm, plan.tf_dense
    tiles_per_seq = d.seq // tm
    row = pl.BlockSpec((tm, dm), lambda i, f: (i, 0))
    return pl.pallas_call(
        _ffn_body,
        grid=(t_total // tm, ff // tf),
        in_specs=[row,
                  pl.BlockSpec((1, 6, dm), lambda i, f: (i // tiles_per_seq, 0, 0)),
                  pl.BlockSpec((1, dm), lambda i, f: (0, 0)),
                  pl.BlockSpec((dm, tf), lambda i, f: (0, f)),
                  pl.BlockSpec((dm, tf), lambda i, f: (0, f)),
                  pl.BlockSpec((tf, dm), lambda i, f: (f, 0))],
        out_specs=row,
        out_shape=jax.ShapeDtypeStruct((t_total, dm), F32),
        scratch_shapes=[pltpu.VMEM((tm, dm), BF16), pltpu.VMEM((tm, dm), F32)],
        compiler_params=_params("arbitrary", "arbitrary"),
        name="dense_swiglu",
    )(x, mod, ng[None], wg, wu, wd)


def _expert_body(te_ref, nv_ref, xs_ref, wg_ref, wu_ref, wd_ref, ys_ref, hbuf, acc_ref):
    i = pl.program_id(0)
    f = pl.program_id(1)

    @pl.when(i < nv_ref[0])
    def _():
        @pl.when(f == 0)
        def _():
            hbuf[...] = xs_ref[...].astype(BF16)

        _swiglu_step(hbuf, wg_ref[0], wu_ref[0], wd_ref[0], acc_ref, f == 0)

        @pl.when(f == pl.num_programs(1) - 1)
        def _():
            ys_ref[...] = acc_ref[...]

    @pl.when((i >= nv_ref[0]) & (f == 0))
    def _():
        ys_ref[...] = jnp.zeros_like(ys_ref)


def _expert_swiglu(xs, tile_expert, n_valid, wg, wu, wd, plan):
    p_rows, dm = xs.shape
    ff = wg.shape[2]
    tg, tf = plan.tg, plan.tf_expert
    nf = ff // tf

    def row_map(i, f, te, nv):
        return (jnp.minimum(i, nv[0] - 1), 0)

    def fsel(i, f, nv):
        return jnp.where(i < nv[0], f, nf - 1)

    grid_spec = pltpu.PrefetchScalarGridSpec(
        num_scalar_prefetch=2,
        grid=(p_rows // tg, nf),
        in_specs=[pl.BlockSpec((tg, dm), row_map),
                  pl.BlockSpec((1, dm, tf), lambda i, f, te, nv: (te[i], 0, fsel(i, f, nv))),
                  pl.BlockSpec((1, dm, tf), lambda i, f, te, nv: (te[i], 0, fsel(i, f, nv))),
                  pl.BlockSpec((1, tf, dm), lambda i, f, te, nv: (te[i], fsel(i, f, nv), 0))],
        out_specs=pl.BlockSpec((tg, dm), lambda i, f, te, nv: (i, 0)),
        scratch_shapes=[pltpu.VMEM((tg, dm), BF16), pltpu.VMEM((tg, dm), F32)],
    )
    return pl.pallas_call(
        _expert_body,
        grid_spec=grid_spec,
        out_shape=jax.ShapeDtypeStruct((p_rows, dm), F32),
        compiler_params=_params("arbitrary", "arbitrary"),
        name="expert_swiglu",
    )(tile_expert, n_valid, xs, wg, wu, wd)


def _qkv_body(x_ref, mod_ref, ng_ref, w_ref, gain_ref, bd_ref, c_ref, s1_ref, s2_ref,
              q_ref, k_ref, v_ref, *, nq, nk, head_dim, half_rot):
    mod = mod_ref[0]
    h = _norm_mod(x_ref[...], ng_ref[...], mod[0:1], mod[1:2]).astype(BF16)
    qkv = _dot(h, w_ref[...])
    v_ref[...] = qkv[:, nq + nk:].astype(BF16)
    cos = c_ref[...]
    sin_hi = s1_ref[...]
    sin_lo = s2_ref[...]
    bd = bd_ref[...]
    for j in range((nq + nk) // V7X_MXU_DIM):
        blk = qkv[:, j * V7X_MXU_DIM:(j + 1) * V7X_MXU_DIM]
        ss = _dot((blk * blk).astype(BF16), bd)
        yn = (blk * lax.rsqrt(ss * (1.0 / head_dim) + EPS)) * gain_ref[:, j * V7X_MXU_DIM:(j + 1) * V7X_MXU_DIM]
        for p in range(V7X_MXU_DIM // V7X_LANES):
            xx = yn[:, p * V7X_LANES:(p + 1) * V7X_LANES]
            rot = (xx * cos + pltpu.roll(xx, half_rot, 1) * sin_hi
                   + pltpu.roll(xx, V7X_LANES - half_rot, 1) * sin_lo).astype(BF16)
            col = j * V7X_MXU_DIM + p * V7X_LANES
            if col < nq:
                q_ref[:, col:col + V7X_LANES] = rot
            else:
                k_ref[:, col - nq:col - nq + V7X_LANES] = rot


def _qkv_project(x, mod, ng, w_qkvd, gain_row, bd, rope, d: Dims, plan):
    t_total, dm = x.shape
    tm = plan.tm
    nq = d.n_heads * d.head_dim
    nk = d.n_kv_heads * 2 * d.head_dim
    tiles_per_seq = d.seq // tm
    row = pl.BlockSpec((tm, dm), lambda i: (i, 0))
    tab = pl.BlockSpec((tm, V7X_LANES), lambda i: (i, 0))
    body = functools.partial(_qkv_body, nq=nq, nk=nk, head_dim=d.head_dim,
                             half_rot=d.head_dim // 8)
    return pl.pallas_call(
        body,
        grid=(t_total // tm,),
        in_specs=[row,
                  pl.BlockSpec((1, 6, dm), lambda i: (i // tiles_per_seq, 0, 0)),
                  _full((1, dm)), _full((dm, nq + 2 * nk)), _full((1, nq + nk)),
                  _full((V7X_MXU_DIM, V7X_MXU_DIM)), tab, tab, tab],
        out_specs=[pl.BlockSpec((tm, nq), lambda i: (i, 0)),
                   pl.BlockSpec((tm, nk), lambda i: (i, 0)),
                   pl.BlockSpec((tm, nk), lambda i: (i, 0))],
        out_shape=[jax.ShapeDtypeStruct((t_total, nq), BF16),
                   jax.ShapeDtypeStruct((t_total, nk), BF16),
                   jax.ShapeDtypeStruct((t_total, nk), BF16)],
        compiler_params=_params("arbitrary"),
        name="qkv_project",
    )(x, mod, ng[None], w_qkvd, gain_row, bd, *rope)


def _attn_body(q_ref, k_ref, v_ref, kp_ref, vp_ref, sink_ref, x_ref, mod_ref, wo_ref, o_ref,
               o_scr, *, n_kv, group, head_dim, tiles_per_seq):
    m = pl.program_id(0)
    tq = q_ref.shape[0]
    blk = ATTN_BLOCK
    qi = lax.broadcasted_iota(I32, (blk, 2 * blk), 0)
    kj = lax.broadcasted_iota(I32, (blk, 2 * blk), 1)
    diff = qi + blk - kj
    in_win = (diff >= 0) & (diff < blk)
    kmin = jnp.where(m % tiles_per_seq == 0, blk, 0)
    lo = lax.broadcasted_iota(I32, (blk, V7X_LANES), 1) < head_dim
    zero = jnp.zeros((blk, V7X_LANES), BF16)
    pairs = group // 2
    for n in range(tq // blk):
        rows = slice(n * blk, (n + 1) * blk)
        if n == 0:
            valid = in_win & (kj >= kmin)
            kprev, vprev = kp_ref[...], vp_ref[...]
        else:
            valid = in_win
            prev = slice((n - 1) * blk, n * blk)
            kprev, vprev = k_ref[prev, :], v_ref[prev, :]
        kcur, vcur = k_ref[rows, :], v_ref[rows, :]
        for g in range(n_kv):
            gl = slice(g * V7X_LANES, (g + 1) * V7X_LANES)
            kk = jnp.concatenate([kprev[:, gl], kcur[:, gl]], axis=0)
            vv = jnp.concatenate([vprev[:, gl], vcur[:, gl]], axis=0)
            slabs = []
            for pr in range(pairs):
                c0 = (g * pairs + pr) * V7X_LANES
                qp = q_ref[rows, c0:c0 + V7X_LANES]
                slabs += [jnp.where(lo, qp, zero), jnp.where(lo, zero, qp)]
            qs = jnp.concatenate(slabs, axis=0)
            s = lax.dot_general(qs, kk, (((1,), (1,)), ((), ())), preferred_element_type=F32)
            ps, dens = [], []
            for hh in range(group):
                sh = jnp.where(valid, s[hh * blk:(hh + 1) * blk, :], MASK_VALUE)
                sink = sink_ref[0, g * group + hh]
                mx = jnp.maximum(jnp.max(sh, axis=-1, keepdims=True), sink)
                p = jnp.exp(sh - mx)
                dens.append(jnp.sum(p, axis=-1, keepdims=True) + jnp.exp(sink - mx))
                ps.append(p.astype(BF16))
            o = _dot(jnp.concatenate(ps, axis=0), vv)
            for pr in range(pairs):
                h0, h1 = 2 * pr, 2 * pr + 1
                oa = o[h0 * blk:(h0 + 1) * blk, :] / dens[h0]
                ob = o[h1 * blk:(h1 + 1) * blk, :] / dens[h1]
                c0 = (g * pairs + pr) * V7X_LANES
                o_scr[rows, c0:c0 + V7X_LANES] = jnp.where(lo, oa, ob).astype(BF16)
    mod = mod_ref[0]
    o_ref[...] = x_ref[...] + mod[2:3] * _dot(o_scr[...], wo_ref[...])


def _attention(q, kd, vd, sinks, x, mod, wo, d: Dims, plan):
    t_total, dm = x.shape
    tq = plan.tm
    nq = q.shape[1]
    nk = kd.shape[1]
    tiles_per_seq = d.seq // tq
    bpt = tq // ATTN_BLOCK
    group = d.n_heads // d.n_kv_heads
    prev = pl.BlockSpec((ATTN_BLOCK, nk), lambda i: (jnp.maximum(i * bpt - 1, 0), 0))
    cur = pl.BlockSpec((tq, nk), lambda i: (i, 0))
    body = functools.partial(_attn_body, n_kv=d.n_kv_heads, group=group, head_dim=d.head_dim,
                             tiles_per_seq=tiles_per_seq)
    return pl.pallas_call(
        body,
        grid=(t_total // tq,),
        in_specs=[pl.BlockSpec((tq, nq), lambda i: (i, 0)), cur, cur, prev, prev,
                  pl.BlockSpec(memory_space=pltpu.SMEM),
                  pl.BlockSpec((tq, dm), lambda i: (i, 0)),
                  pl.BlockSpec((1, 6, dm), lambda i: (i // tiles_per_seq, 0, 0)),
                  _full((nq, dm))],
        out_specs=pl.BlockSpec((tq, dm), lambda i: (i, 0)),
        out_shape=jax.ShapeDtypeStruct((t_total, dm), F32),
        scratch_shapes=[pltpu.VMEM((tq, nq), BF16)],
        compiler_params=_params("arbitrary"),
        name="swa_attention",
    )(q, kd, vd, kd, vd, sinks[None], x, mod, wo)


def _router_body(x_ref, mod_ref, ng_ref, wr_ref, br_ref, ltri_ref, h_ref, mi_ref, mf_ref, cnt_ref,
                 carry, *, n_experts):
    m = pl.program_id(0)
    tt = x_ref.shape[0]
    mod = mod_ref[0]
    h2 = _norm_mod(x_ref[...], ng_ref[...], mod[3:4], mod[4:5])
    h_ref[...] = h2
    logits = jnp.dot(h2, wr_ref[...], preferred_element_type=F32,
                     precision=lax.Precision.HIGHEST) + br_ref[...]
    lane = lax.broadcasted_iota(I32, (tt, V7X_LANES), 1).astype(F32)
    neg = jnp.float32(-jnp.inf)
    logits = jnp.where(lane < n_experts, logits, neg)
    big = jnp.float32(V7X_LANES)
    m1 = jnp.max(logits, axis=-1, keepdims=True)
    i1 = jnp.min(jnp.where(logits == m1, lane, big), axis=-1, keepdims=True)
    rest = jnp.where(lane == i1, neg, logits)
    m2 = jnp.max(rest, axis=-1, keepdims=True)
    i2 = jnp.min(jnp.where(rest == m2, lane, big), axis=-1, keepdims=True)
    e2 = jnp.exp(m2 - m1)
    w1 = 1.0 / (1.0 + e2)
    w2 = e2 / (1.0 + e2)

    @pl.when(m == 0)
    def _():
        carry[...] = jnp.zeros_like(carry)

    sel1 = lane == i1
    sel2 = lane == i2
    assigned = jnp.where(sel1 | sel2, 1.0, 0.0)
    before = _dot(ltri_ref[...], assigned.astype(BF16)) + carry[0:1, :]
    r1 = jnp.sum(jnp.where(sel1, before, 0.0), axis=-1, keepdims=True)
    r2 = jnp.sum(jnp.where(sel2, before, 0.0), axis=-1, keepdims=True)
    total = carry[0:1, :] + jnp.sum(assigned, axis=0, keepdims=True)
    carry[...] = jnp.broadcast_to(total, carry.shape)
    cnt_ref[...] = jnp.broadcast_to(total, cnt_ref.shape)
    zf = jnp.zeros((tt, V7X_LANES), F32)
    mi = jnp.where(lane == 0, i1, jnp.where(lane == 1, i2, jnp.where(lane == 2, r1, jnp.where(lane == 3, r2, zf))))
    mi_ref[...] = mi.astype(I32)
    mf_ref[...] = jnp.where(lane == 0, w1, jnp.where(lane == 1, w2, zf))


def _route(x, mod, ng, w_router, b_router, d: Dims, plan):
    t_total, dm = x.shape
    tt = plan.tm
    ne = d.n_experts
    tiles_per_seq = d.seq // tt
    wr = jnp.zeros((dm, V7X_LANES), F32).at[:, :ne].set(w_router)
    br = jnp.zeros((1, V7X_LANES), F32).at[0, :ne].set(b_router)
    ltri = jnp.tril(jnp.ones((tt, tt), BF16), -1)
    row = pl.BlockSpec((tt, dm), lambda i: (i, 0))
    meta = pl.BlockSpec((tt, V7X_LANES), lambda i: (i, 0))
    return pl.pallas_call(
        functools.partial(_router_body, n_experts=ne),
        grid=(t_total // tt,),
        in_specs=[row,
                  pl.BlockSpec((1, 6, dm), lambda i: (i // tiles_per_seq, 0, 0)),
                  _full((1, dm)), _full((dm, V7X_LANES)), _full((1, V7X_LANES)), _full((tt, tt))],
        out_specs=[row, meta, meta, _full((V7X_SUBLANES, V7X_LANES))],
        out_shape=[jax.ShapeDtypeStruct((t_total, dm), F32),
                   jax.ShapeDtypeStruct((t_total, V7X_LANES), I32),
                   jax.ShapeDtypeStruct((t_total, V7X_LANES), F32),
                   jax.ShapeDtypeStruct((V7X_SUBLANES, V7X_LANES), F32)],
        scratch_shapes=[pltpu.VMEM((V7X_SUBLANES, V7X_LANES), F32)],
        compiler_params=_params("arbitrary"),
        name="moe_router",
    )(x, mod, ng[None], wr, br, ltri)


def _row_copy(src_hbm, row, dst, i, sem):
    return pltpu.make_async_copy(src_hbm.at[pl.ds(row, 1), :], dst.at[pl.ds(i, 1), :], sem)


def _gather_body(idx_ref, src_hbm, o_ref, sem):
    n = o_ref.shape[0]

    def issue(i, c):
        _row_copy(src_hbm, idx_ref[0, 0, i], o_ref, i, sem).start()
        return c

    lax.fori_loop(0, n, issue, 0)

    def drain(i, c):
        _row_copy(src_hbm, idx_ref[0, 0, i], o_ref, i, sem).wait()
        return c

    lax.fori_loop(0, n, drain, 0)


def _gather_rows(src, row_of_slot, plan):
    p_rows = row_of_slot.shape[0]
    dm = src.shape[1]
    tg = plan.tg
    return pl.pallas_call(
        _gather_body,
        grid=(p_rows // tg,),
        in_specs=[pl.BlockSpec((1, 1, tg), lambda i: (i, 0, 0), memory_space=pltpu.SMEM),
                  pl.BlockSpec(memory_space=pl.ANY)],
        out_specs=pl.BlockSpec((tg, dm), lambda i: (i, 0)),
        out_shape=jax.ShapeDtypeStruct((p_rows, dm), src.dtype),
        scratch_shapes=[pltpu.SemaphoreType.DMA(())],
        compiler_params=_params("arbitrary"),
        name="moe_dispatch",
    )(row_of_slot.reshape(p_rows // tg, 1, tg), src)


def _combine_body(p1_ref, p2_ref, ys_hbm, x_ref, mod_ref, mf_ref, o_ref, y1, y2, sem):
    n = x_ref.shape[0]

    def issue(i, c):
        _row_copy(ys_hbm, p1_ref[0, 0, i], y1, i, sem.at[0]).start()
        _row_copy(ys_hbm, p2_ref[0, 0, i], y2, i, sem.at[1]).start()
        return c

    lax.fori_loop(0, n, issue, 0)

    def drain(i, c):
        _row_copy(ys_hbm, p1_ref[0, 0, i], y1, i, sem.at[0]).wait()
        _row_copy(ys_hbm, p2_ref[0, 0, i], y2, i, sem.at[1]).wait()
        return c

    lax.fori_loop(0, n, drain, 0)
    mf = mf_ref[...]
    mod = mod_ref[0]
    y = mf[:, 0:1] * y1[...] + mf[:, 1:2] * y2[...]
    o_ref[...] = x_ref[...] + mod[5:6] * y


def _combine(ys, pos1, pos2, x, mod, meta_f, d: Dims, plan):
    t_total, dm = x.shape
    tc = plan.tm
    tiles_per_seq = d.seq // tc
    idx = pl.BlockSpec((1, 1, tc), lambda i: (i, 0, 0), memory_space=pltpu.SMEM)
    row = pl.BlockSpec((tc, dm), lambda i: (i, 0))
    return pl.pallas_call(
        _combine_body,
        grid=(t_total // tc,),
        in_specs=[idx, idx, pl.BlockSpec(memory_space=pl.ANY), row,
                  pl.BlockSpec((1, 6, dm), lambda i: (i // tiles_per_seq, 0, 0)),
                  pl.BlockSpec((tc, V7X_LANES), lambda i: (i, 0))],
        out_specs=row,
        out_shape=jax.ShapeDtypeStruct((t_total, dm), F32),
        scratch_shapes=[pltpu.VMEM((tc, dm), F32), pltpu.VMEM((tc, dm), F32),
                        pltpu.SemaphoreType.DMA((2,))],
        compiler_params=_params("arbitrary"),
        name="moe_combine",
    )(pos1.reshape(t_total // tc, 1, tc), pos2.reshape(t_total // tc, 1, tc), ys, x, mod, meta_f)


def _moe_sublayer(x, mod, ng, w_router, b_router, wg, wu, wd, d: Dims, plan):
    t_total, dm = x.shape
    ne, tg = d.n_experts, plan.tg
    h2, meta_i, meta_f, cnt = _route(x, mod, ng, w_router, b_router, d, plan)
    counts = cnt[0, :ne].astype(I32)
    padded = (counts + tg - 1) // tg * tg
    ends = jnp.cumsum(padded)
    starts = ends - padded
    expert = meta_i[:, 0:TOP_K]
    pos = starts[expert] + meta_i[:, TOP_K:2 * TOP_K]
    n_slots = -(-TOP_K * t_total // tg) * tg + ne * tg
    n_tiles = n_slots // tg
    token = jnp.arange(TOP_K * t_total, dtype=I32) // TOP_K
    token_of_slot = jnp.zeros((n_slots,), I32).at[pos.reshape(-1)].set(token)
    n_valid = (ends[ne - 1] // tg).astype(I32)
    tile_start = jnp.minimum(jnp.arange(n_tiles, dtype=I32), n_valid - 1) * tg
    tile_expert = jnp.minimum(jnp.searchsorted(ends, tile_start, side="right"), ne - 1).astype(I32)
    xs = _gather_rows(h2, token_of_slot, plan)
    ys = _expert_swiglu(xs, tile_expert, n_valid.reshape(1), wg, wu, wd, plan)
    return _combine(ys, pos[:, 0], pos[:, 1], x, mod, meta_f, d, plan)


def kernel(x, c, positions, norm_g, w_ada, b_ada, conv_w_pw1, conv_b_pw1, conv_w_dw, conv_b_dw, conv_ln_g, conv_ln_b, conv_w_pw2, conv_b_pw2, attn_w_qkv, attn_q_gain, attn_k_gain, attn_sinks, attn_w_o, ffn_w_gate, ffn_w_up, ffn_w_down, moe_w_router, moe_b_router, moe_w_gate, moe_w_up, moe_w_down):
    bsz, seq, dm = x.shape
    head_dim = attn_q_gain.shape[1]
    n_heads = attn_sinks.shape[1]
    n_kv = (attn_w_qkv.shape[2] // head_dim - n_heads) // 2
    d = Dims(batch=bsz, seq=seq, d_model=dm, depth=norm_g.shape[0], conv_kernel=conv_w_dw.shape[1],
             n_heads=n_heads, n_kv_heads=n_kv, head_dim=head_dim, d_ff_dense=ffn_w_gate.shape[2],
             n_experts=moe_w_router.shape[2], d_ff_expert=moe_w_gate.shape[3])
    assert 2 * head_dim == V7X_LANES and (n_heads // n_kv) % 2 == 0 and n_heads % n_kv == 0
    assert seq % ATTN_BLOCK == 0 and dm % V7X_MXU_DIM == 0
    assert (n_heads + 2 * n_kv) * head_dim % V7X_MXU_DIM == 0 and n_kv * 2 * head_dim % V7X_MXU_DIM == 0
    plan = _make_plan(d)
    t_total = bsz * seq
    nq = n_heads * head_dim

    mod_all = _ada_all_layers(c, w_ada, b_ada, plan).reshape(d.depth, bsz, 6, dm)
    rope = _rope_tables(positions, d, plan)
    blk_id = jnp.arange(V7X_MXU_DIM) // head_dim
    head_block_ones = (blk_id[:, None] == blk_id[None, :]).astype(BF16)

    xt = x.reshape(t_total, dm)
    for i in range(d.depth):
        j = i // 2
        mod = mod_all[i]
        if i % 2 == 0:
            xt = _conv_sublayer(xt, mod, norm_g[i, 0], conv_w_pw1[j].astype(BF16), conv_b_pw1[j],
                                conv_w_dw[j], conv_b_dw[j], conv_ln_g[j], conv_ln_b[j],
                                conv_w_pw2[j].astype(BF16), conv_b_pw2[j], d, plan)
            xt = _ffn_sublayer(xt, mod, norm_g[i, 1], ffn_w_gate[j].astype(BF16),
                               ffn_w_up[j].astype(BF16), ffn_w_down[j].astype(BF16), d, plan)
        else:
            wqkv = attn_w_qkv[j]
            nkv = n_kv * head_dim
            dup = lambda w: jnp.repeat(w.reshape(dm, n_kv, 1, head_dim), 2, axis=2).reshape(dm, 2 * nkv)
            w_qkvd = jnp.concatenate([wqkv[:, :nq], dup(wqkv[:, nq:nq + nkv]), dup(wqkv[:, nq + nkv:])],
                                     axis=1).astype(BF16)
            gain_row = jnp.concatenate([jnp.tile(attn_q_gain[j] * head_dim ** -0.5, n_heads),
                                        jnp.tile(attn_k_gain[j], 2 * n_kv)])[None]
            q, kd, vd = _qkv_project(xt, mod, norm_g[i, 0], w_qkvd, gain_row, head_block_ones, rope, d, plan)
            xt = _attention(q, kd, vd, attn_sinks[j], xt, mod, attn_w_o[j].astype(BF16), d, plan)
            xt = _moe_sublayer(xt, mod, norm_g[i, 1], moe_w_router[j], moe_b_router[j],
                               moe_w_gate[j].astype(BF16), moe_w_up[j].astype(BF16),
                               moe_w_down[j].astype(BF16), d, plan)
    return xt.reshape(bsz, seq, dm)
```

```python
import functools
from typing import NamedTuple

import jax
import jax.numpy as jnp
from jax import lax
from jax.experimental import pallas as pl
from jax.experimental.pallas import tpu as pltpu

F32 = jnp.float32
BF16 = jnp.bfloat16
I32 = jnp.int32

V7X_LANES = 128
V7X_SUBLANES = 8
V7X_MXU_DIM = 256
ROW_ALIGN = 2 * V7X_SUBLANES
V7X_VMEM_LIMIT_BYTES = 56 * 1024 * 1024

ROPE_THETA = 500000.0
EPS = 1e-6
ATTN_BLOCK = 128
TOP_K = 2
MASK_VALUE = -1e30


class Dims(NamedTuple):
    batch: int
    seq: int
    d_model: int
    depth: int
    conv_kernel: int
    n_heads: int
    n_kv_heads: int
    head_dim: int
    d_ff_dense: int
    n_experts: int
    d_ff_expert: int


class Plan(NamedTuple):
    tm: int
    tf_dense: int
    tf_expert: int
    tg: int
    tn_ada: int
    conv_rows: int
    conv_cols: int
    chunk: int


def _largest_tile(total, target, quantum):
    best = None
    t = quantum
    while t <= min(total, target):
        if total % t == 0:
            best = t
        t += quantum
    return best if best is not None else total


def _make_plan(d: Dims) -> Plan:
    tm = _largest_tile(d.seq, 512, ATTN_BLOCK)
    return Plan(
        tm=tm,
        tf_dense=_largest_tile(d.d_ff_dense, 1408, V7X_LANES),
        tf_expert=_largest_tile(d.d_ff_expert, 896, V7X_LANES),
        tg=_largest_tile(d.batch * d.seq, 1024, ATTN_BLOCK),
        tn_ada=_largest_tile(6 * d.d_model, 1536, V7X_LANES),
        conv_rows=64,
        conv_cols=V7X_LANES,
        chunk=ATTN_BLOCK,
    )


def _params(*semantics):
    return pltpu.CompilerParams(dimension_semantics=semantics,
                                vmem_limit_bytes=V7X_VMEM_LIMIT_BYTES)


def _dot(a, b):
    return jnp.dot(a, b, preferred_element_type=F32)


def _sigmoid(x):
    return 1.0 / (1.0 + jnp.exp(-x))


def _norm_mod(x, g_row, shift_row, scale_row):
    y = x * lax.rsqrt(jnp.mean(x * x, axis=-1, keepdims=True) + EPS)
    return (y * g_row) * (1.0 + scale_row) + shift_row


def _full(shape):
    return pl.BlockSpec(shape, lambda *_: (0,) * len(shape))


def _ada_body(c_ref, w_ref, b_ref, o_ref):
    c = c_ref[...]
    ca = c * _sigmoid(c)
    o_ref[0] = jnp.dot(ca, w_ref[0], preferred_element_type=F32,
                       precision=lax.Precision.HIGHEST) + b_ref[0]


def _ada_all_layers(c, w_ada, b_ada, plan):
    depth, dm, n6 = w_ada.shape
    bsz = c.shape[0]
    tn = plan.tn_ada
    return pl.pallas_call(
        _ada_body,
        grid=(depth, n6 // tn),
        in_specs=[_full((bsz, dm)),
                  pl.BlockSpec((1, dm, tn), lambda i, j: (i, 0, j)),
                  pl.BlockSpec((1, 1, tn), lambda i, j: (i, 0, j))],
        out_specs=pl.BlockSpec((1, bsz, tn), lambda i, j: (i, 0, j)),
        out_shape=jax.ShapeDtypeStruct((depth, bsz, n6), F32),
        compiler_params=_params("arbitrary", "arbitrary"),
        name="adaln_mod",
    )(c, w_ada, b_ada.reshape(depth, 1, n6))


def _rope_body(pos_ref, f_ref, m1_ref, m2_ref, c_ref, s1_ref, s2_ref):
    ang = pos_ref[...].astype(F32) * f_ref[...]
    s = jnp.sin(ang)
    c_ref[...] = jnp.cos(ang)
    s1_ref[...] = s * m1_ref[...]
    s2_ref[...] = s * m2_ref[...]


def _rope_tables(positions, d: Dims, plan):
    t_total = d.batch * d.seq
    rot = d.head_dim // 4
    half = rot // 2
    inv_freq = ROPE_THETA ** (-jnp.arange(0, rot, 2, dtype=F32) / rot)
    lane = jnp.arange(V7X_LANES)
    in_head = lane % d.head_dim
    freq_row = jnp.where(in_head < rot, inv_freq[in_head % half], 0.0).astype(F32)[None]
    m1 = jnp.where((in_head >= half) & (in_head < rot), 1.0, 0.0).astype(F32)[None]
    m2 = jnp.where(in_head < half, -1.0, 0.0).astype(F32)[None]
    tm = plan.tm
    row = pl.BlockSpec((tm, V7X_LANES), lambda i: (i, 0))
    tab = jax.ShapeDtypeStruct((t_total, V7X_LANES), F32)
    return pl.pallas_call(
        _rope_body,
        grid=(t_total // tm,),
        in_specs=[pl.BlockSpec((tm, 1), lambda i: (i, 0)),
                  _full((1, V7X_LANES)), _full((1, V7X_LANES)), _full((1, V7X_LANES))],
        out_specs=[row, row, row],
        out_shape=[tab, tab, tab],
        compiler_params=_params("arbitrary"),
        name="rope_tables",
    )(positions.reshape(t_total, 1), freq_row, m1, m2)


def _conv_body(x_ref, mod_ref, ng_ref, w1_ref, b1_ref, wd_ref, bd_ref, lg_ref, lb_ref,
               w2_ref, b2_ref, o_ref, ubuf, vbuf, *, dm, ktaps, halo, tiles_per_seq, rc, cb):
    m = pl.program_id(0)
    tm = x_ref.shape[0]
    x = x_ref[...]
    mod = mod_ref[0]
    h = _norm_mod(x, ng_ref[...], mod[0:1], mod[1:2]).astype(BF16)
    u = _dot(h, w1_ref[...]) + b1_ref[...]
    glu = u[:, :dm] * _sigmoid(u[:, dm:])

    @pl.when(m % tiles_per_seq == 0)
    def _():
        ubuf[0:halo, :] = jnp.zeros((halo, dm), F32)

    ubuf[halo:halo + tm, :] = glu

    def chunk(ci, carry):
        r0 = pl.multiple_of(ci * rc, rc)
        pieces = []
        for cbi in range(dm // cb):
            cols = slice(cbi * cb, (cbi + 1) * cb)
            win = ubuf[pl.ds(r0, rc + halo), cols]
            acc = None
            for r in range(V7X_SUBLANES):
                z = None
                for q in range(halo // V7X_SUBLANES):
                    dl = V7X_SUBLANES * q + r
                    if dl >= ktaps:
                        continue
                    s = halo - V7X_SUBLANES * (q + 1)
                    term = wd_ref[dl:dl + 1, cols] * win[s:s + rc + V7X_SUBLANES, :]
                    z = term if z is None else z + term
                if z is None:
                    continue
                if r:
                    z = pltpu.roll(z, r, 0)
                z = z[V7X_SUBLANES:V7X_SUBLANES + rc, :]
                acc = z if acc is None else acc + z
            pieces.append(acc)
        conv = jnp.concatenate(pieces, axis=1) + bd_ref[...]
        mu = jnp.mean(conv, axis=-1, keepdims=True)
        cen = conv - mu
        var = jnp.mean(cen * cen, axis=-1, keepdims=True)
        y = cen * lax.rsqrt(var + EPS) * lg_ref[...] + lb_ref[...]
        vbuf[pl.ds(r0, rc), :] = (y * _sigmoid(y)).astype(BF16)
        return carry

    lax.fori_loop(0, tm // rc, chunk, 0)
    ubuf[0:halo, :] = ubuf[tm:tm + halo, :]
    y2 = _dot(vbuf[...], w2_ref[...]) + b2_ref[...]
    o_ref[...] = x + mod[2:3] * y2


def _conv_sublayer(x, mod, ng, w1, b1, w_dw, b_dw, ln_g, ln_b, w2, b2, d: Dims, plan):
    t_total, dm = x.shape
    tm = plan.tm
    ktaps = w_dw.shape[0]
    halo = -(-ktaps // V7X_SUBLANES) * V7X_SUBLANES
    wd = jnp.zeros((halo, dm), F32).at[:ktaps].set(w_dw[::-1])
    tiles_per_seq = d.seq // tm
    row = pl.BlockSpec((tm, dm), lambda i: (i, 0))
    body = functools.partial(_conv_body, dm=dm, ktaps=ktaps, halo=halo,
                             tiles_per_seq=tiles_per_seq, rc=plan.conv_rows, cb=plan.conv_cols)
    return pl.pallas_call(
        body,
        grid=(t_total // tm,),
        in_specs=[row,
                  pl.BlockSpec((1, 6, dm), lambda i: (i // tiles_per_seq, 0, 0)),
                  _full((1, dm)), _full((dm, 2 * dm)), _full((1, 2 * dm)),
                  _full(wd.shape), _full((1, dm)), _full((1, dm)), _full((1, dm)),
                  _full((dm, dm)), _full((1, dm))],
        out_specs=row,
        out_shape=jax.ShapeDtypeStruct((t_total, dm), F32),
        scratch_shapes=[pltpu.VMEM((halo + tm, dm), F32), pltpu.VMEM((tm, dm), BF16)],
        compiler_params=_params("arbitrary"),
        name="conv_sublayer",
    )(x, mod, ng[None], w1, b1[None], wd, b_dw[None], ln_g[None], ln_b[None], w2, b2[None])


def _swiglu_step(hbuf, wg, wu, wd, acc_ref, first):
    hh = hbuf[...]
    g = _dot(hh, wg)
    u = _dot(hh, wu)
    a = ((g * _sigmoid(g)) * u).astype(BF16)
    part = _dot(a, wd)

    @pl.when(first)
    def _():
        acc_ref[...] = part

    @pl.when(jnp.logical_not(first))
    def _():
        acc_ref[...] += part


def _ffn_body(x_ref, mod_ref, ng_ref, wg_ref, wu_ref, wd_ref, o_ref, hbuf, acc_ref):
    f = pl.program_id(1)
    mod = mod_ref[0]

    @pl.when(f == 0)
    def _():
        hbuf[...] = _norm_mod(x_ref[...], ng_ref[...], mod[3:4], mod[4:5]).astype(BF16)

    _swiglu_step(hbuf, wg_ref[...], wu_ref[...], wd_ref[...], acc_ref, f == 0)

    @pl.when(f == pl.num_programs(1) - 1)
    def _():
        o_ref[...] = x_ref[...] + mod[5:6] * acc_ref[...]


def _ffn_sublayer(x, mod, ng, wg, wu, wd, d: Dims, plan):
    t_total, dm = x.shape
    ff = wg.shape[1]
    tm, tf = plan.tm, plan.tf_dense
    tiles_per_seq = d.seq // tm
    row = pl.BlockSpec((tm, dm), lambda i, f: (i, 0))
    return pl.pallas_call(
        _ffn_body,
        grid=(t_total // tm, ff // tf),
        in_specs=[row,
                  pl.BlockSpec((1, 6, dm), lambda i, f: (i // tiles_per_seq, 0, 0)),
                  pl.BlockSpec((1, dm), lambda i, f: (0, 0)),
                  pl.BlockSpec((dm, tf), lambda i, f: (0, f)),
                  pl.BlockSpec((dm, tf), lambda i, f: (0, f)),
                  pl.BlockSpec((tf, dm), lambda i, f: (f, 0))],
        out_specs=row,
        out_shape=jax.ShapeDtypeStruct((t_total, dm), F32),
        scratch_shapes=[pltpu.VMEM((tm, dm), BF16), pltpu.VMEM((tm, dm), F32)],
        compiler_params=_params("arbitrary", "arbitrary"),
        name="dense_swiglu",
    )(x, mod, ng[None], wg, wu, wd)


def _expert_body(te_ref, nv_ref, xs_ref, wg_ref, wu_ref, wd_ref, ys_ref, acc_ref):
    i = pl.program_id(0)
    f = pl.program_id(1)

    @pl.when(i < nv_ref[0])
    def _():
        _swiglu_step(xs_ref, wg_ref[0], wu_ref[0], wd_ref[0], acc_ref, f == 0)

        @pl.when(f == pl.num_programs(1) - 1)
        def _():
            ys_ref[...] = acc_ref[...].astype(ys_ref.dtype)

    @pl.when((i >= nv_ref[0]) & (f == 0))
    def _():
        ys_ref[...] = jnp.zeros_like(ys_ref)


def _expert_swiglu(xs, tile_expert, n_valid, n_tiles, wg, wu, wd, plan):
    dm = xs.shape[1]
    ff = wg.shape[2]
    tg, tf = plan.tg, plan.tf_expert
    p_rows = n_tiles * tg
    nf = ff // tf

    def row_map(i, f, te, nv):
        return (jnp.minimum(i, nv[0] - 1), 0)

    def fsel(i, f, nv):
        return jnp.where(i < nv[0], f, nf - 1)

    grid_spec = pltpu.PrefetchScalarGridSpec(
        num_scalar_prefetch=2,
        grid=(p_rows // tg, nf),
        in_specs=[pl.BlockSpec((tg, dm), row_map),
                  pl.BlockSpec((1, dm, tf), lambda i, f, te, nv: (te[i], 0, fsel(i, f, nv))),
                  pl.BlockSpec((1, dm, tf), lambda i, f, te, nv: (te[i], 0, fsel(i, f, nv))),
                  pl.BlockSpec((1, tf, dm), lambda i, f, te, nv: (te[i], fsel(i, f, nv), 0))],
        out_specs=pl.BlockSpec((tg, dm), lambda i, f, te, nv: (i, 0)),
        scratch_shapes=[pltpu.VMEM((tg, dm), F32)],
    )
    return pl.pallas_call(
        _expert_body,
        grid_spec=grid_spec,
        out_shape=jax.ShapeDtypeStruct((p_rows, dm), BF16),
        compiler_params=_params("arbitrary", "arbitrary"),
        name="expert_swiglu",
    )(tile_expert, n_valid, xs, wg, wu, wd)


def _qkv_body(x_ref, mod_ref, ng_ref, w_ref, gain_ref, bd_ref, c_ref, s1_ref, s2_ref,
              q_ref, k_ref, v_ref, *, nq, nk, head_dim, half_rot):
    mod = mod_ref[0]
    h = _norm_mod(x_ref[...], ng_ref[...], mod[0:1], mod[1:2]).astype(BF16)
    qkv = _dot(h, w_ref[...])
    v_ref[...] = qkv[:, nq + nk:].astype(BF16)
    cos = c_ref[...]
    sin_hi = s1_ref[...]
    sin_lo = s2_ref[...]
    bd = bd_ref[...]
    for j in range((nq + nk) // V7X_MXU_DIM):
        blk = qkv[:, j * V7X_MXU_DIM:(j + 1) * V7X_MXU_DIM]
        ss = _dot((blk * blk).astype(BF16), bd)
        yn = (blk * lax.rsqrt(ss * (1.0 / head_dim) + EPS)) * gain_ref[:, j * V7X_MXU_DIM:(j + 1) * V7X_MXU_DIM]
        for p in range(V7X_MXU_DIM // V7X_LANES):
            xx = yn[:, p * V7X_LANES:(p + 1) * V7X_LANES]
            rot = (xx * cos + pltpu.roll(xx, half_rot, 1) * sin_hi
                   + pltpu.roll(xx, V7X_LANES - half_rot, 1) * sin_lo).astype(BF16)
            col = j * V7X_MXU_DIM + p * V7X_LANES
            if col < nq:
                q_ref[:, col:col + V7X_LANES] = rot
            else:
                k_ref[:, col - nq:col - nq + V7X_LANES] = rot


def _qkv_project(x, mod, ng, w_qkvd, gain_row, bd, rope, d: Dims, plan):
    t_total, dm = x.shape
    tm = plan.tm
    nq = d.n_heads * d.head_dim
    nk = d.n_kv_heads * 2 * d.head_dim
    tiles_per_seq = d.seq // tm
    row = pl.BlockSpec((tm, dm), lambda i: (i, 0))
    tab = pl.BlockSpec((tm, V7X_LANES), lambda i: (i, 0))
    body = functools.partial(_qkv_body, nq=nq, nk=nk, head_dim=d.head_dim,
                             half_rot=d.head_dim // 8)
    return pl.pallas_call(
        body,
        grid=(t_total // tm,),
        in_specs=[row,
                  pl.BlockSpec((1, 6, dm), lambda i: (i // tiles_per_seq, 0, 0)),
                  _full((1, dm)), _full((dm, nq + 2 * nk)), _full((1, nq + nk)),
                  _full((V7X_MXU_DIM, V7X_MXU_DIM)), tab, tab, tab],
        out_specs=[pl.BlockSpec((tm, nq), lambda i: (i, 0)),
                   pl.BlockSpec((tm, nk), lambda i: (i, 0)),
                   pl.BlockSpec((tm, nk), lambda i: (i, 0))],
        out_shape=[jax.ShapeDtypeStruct((t_total, nq), BF16),
                   jax.ShapeDtypeStruct((t_total, nk), BF16),
                   jax.ShapeDtypeStruct((t_total, nk), BF16)],
        compiler_params=_params("arbitrary"),
        name="qkv_project",
    )(x, mod, ng[None], w_qkvd, gain_row, bd, *rope)


def _attn_body(q_ref, k_ref, v_ref, kp_ref, vp_ref, sink_ref, x_ref, mod_ref, wo_ref, o_ref,
               o_scr, *, n_kv, group, head_dim, tiles_per_seq):
    m = pl.program_id(0)
    tq = q_ref.shape[0]
    blk = ATTN_BLOCK
    qi = lax.broadcasted_iota(I32, (blk, 2 * blk), 0)
    kj = lax.broadcasted_iota(I32, (blk, 2 * blk), 1)
    diff = qi + blk - kj
    in_win = (diff >= 0) & (diff < blk)
    kmin = jnp.where(m % tiles_per_seq == 0, blk, 0)
    lo = lax.broadcasted_iota(I32, (blk, V7X_LANES), 1) < head_dim
    zero = jnp.zeros((blk, V7X_LANES), BF16)
    pairs = group // 2
    for n in range(tq // blk):
        rows = slice(n * blk, (n + 1) * blk)
        if n == 0:
            valid = in_win & (kj >= kmin)
            kprev, vprev = kp_ref[...], vp_ref[...]
        else:
            valid = in_win
            prev = slice((n - 1) * blk, n * blk)
            kprev, vprev = k_ref[prev, :], v_ref[prev, :]
        kcur, vcur = k_ref[rows, :], v_ref[rows, :]
        for g in range(n_kv):
            gl = slice(g * V7X_LANES, (g + 1) * V7X_LANES)
            kk = jnp.concatenate([kprev[:, gl], kcur[:, gl]], axis=0)
            vv = jnp.concatenate([vprev[:, gl], vcur[:, gl]], axis=0)
            slabs = []
            for pr in range(pairs):
                c0 = (g * pairs + pr) * V7X_LANES
                qp = q_ref[rows, c0:c0 + V7X_LANES]
                slabs += [jnp.where(lo, qp, zero), jnp.where(lo, zero, qp)]
            qs = jnp.concatenate(slabs, axis=0)
            s = lax.dot_general(qs, kk, (((1,), (1,)), ((), ())), preferred_element_type=F32)
            ps, dens = [], []
            for hh in range(group):
                sh = jnp.where(valid, s[hh * blk:(hh + 1) * blk, :], MASK_VALUE)
                sink = sink_ref[0, g * group + hh]
                mx = jnp.maximum(jnp.max(sh, axis=-1, keepdims=True), sink)
                p = jnp.exp(sh - mx)
                dens.append(jnp.sum(p, axis=-1, keepdims=True) + jnp.exp(sink - mx))
                ps.append(p.astype(BF16))
            o = _dot(jnp.concatenate(ps, axis=0), vv)
            for pr in range(pairs):
                h0, h1 = 2 * pr, 2 * pr + 1
                oa = o[h0 * blk:(h0 + 1) * blk, :] / dens[h0]
                ob = o[h1 * blk:(h1 + 1) * blk, :] / dens[h1]
                c0 = (g * pairs + pr) * V7X_LANES
                o_scr[rows, c0:c0 + V7X_LANES] = jnp.where(lo, oa, ob).astype(BF16)
    mod = mod_ref[0]
    o_ref[...] = x_ref[...] + mod[2:3] * _dot(o_scr[...], wo_ref[...])


def _attention(q, kd, vd, sinks, x, mod, wo, d: Dims, plan):
    t_total, dm = x.shape
    tq = plan.tm
    nq = q.shape[1]
    nk = kd.shape[1]
    tiles_per_seq = d.seq // tq
    bpt = tq // ATTN_BLOCK
    group = d.n_heads // d.n_kv_heads
    prev = pl.BlockSpec((ATTN_BLOCK, nk), lambda i: (jnp.maximum(i * bpt - 1, 0), 0))
    cur = pl.BlockSpec((tq, nk), lambda i: (i, 0))
    body = functools.partial(_attn_body, n_kv=d.n_kv_heads, group=group, head_dim=d.head_dim,
                             tiles_per_seq=tiles_per_seq)
    return pl.pallas_call(
        body,
        grid=(t_total // tq,),
        in_specs=[pl.BlockSpec((tq, nq), lambda i: (i, 0)), cur, cur, prev, prev,
                  pl.BlockSpec(memory_space=pltpu.SMEM),
                  pl.BlockSpec((tq, dm), lambda i: (i, 0)),
                  pl.BlockSpec((1, 6, dm), lambda i: (i // tiles_per_seq, 0, 0)),
                  _full((nq, dm))],
        out_specs=pl.BlockSpec((tq, dm), lambda i: (i, 0)),
        out_shape=jax.ShapeDtypeStruct((t_total, dm), F32),
        scratch_shapes=[pltpu.VMEM((tq, nq), BF16)],
        compiler_params=_params("arbitrary"),
        name="swa_attention",
    )(q, kd, vd, kd, vd, sinks[None], x, mod, wo)


def _router_body(x_ref, mod_ref, ng_ref, wr_ref, br_ref, ltri_ref, h_ref, meta_ref, cnt_ref,
                 *, n_experts):
    tt = x_ref.shape[0]
    mod = mod_ref[0]
    h2 = _norm_mod(x_ref[...], ng_ref[...], mod[3:4], mod[4:5])
    h_ref[...] = h2.astype(BF16)
    logits = jnp.dot(h2, wr_ref[...], preferred_element_type=F32,
                     precision=lax.Precision.HIGHEST) + br_ref[...]
    lane = lax.broadcasted_iota(I32, (tt, V7X_LANES), 1).astype(F32)
    neg = jnp.float32(-jnp.inf)
    logits = jnp.where(lane < n_experts, logits, neg)
    big = jnp.float32(V7X_LANES)
    m1 = jnp.max(logits, axis=-1, keepdims=True)
    i1 = jnp.min(jnp.where(logits == m1, lane, big), axis=-1, keepdims=True)
    rest = jnp.where(lane == i1, neg, logits)
    m2 = jnp.max(rest, axis=-1, keepdims=True)
    i2 = jnp.min(jnp.where(rest == m2, lane, big), axis=-1, keepdims=True)
    e2 = jnp.exp(m2 - m1)
    w1 = 1.0 / (1.0 + e2)
    w2 = e2 / (1.0 + e2)

    sel1 = lane == i1
    sel2 = lane == i2
    assigned = jnp.where(sel1 | sel2, 1.0, 0.0)
    before = _dot(ltri_ref[...], assigned.astype(BF16))
    r1 = jnp.sum(jnp.where(sel1, before, 0.0), axis=-1, keepdims=True)
    r2 = jnp.sum(jnp.where(sel2, before, 0.0), axis=-1, keepdims=True)
    cnt_ref[0] = jnp.broadcast_to(jnp.sum(assigned, axis=0, keepdims=True), cnt_ref.shape[1:])
    meta = jnp.zeros((tt, V7X_LANES), F32)
    for col, val in enumerate((i1, i2, r1, r2, w1, w2)):
        meta = jnp.where(lane == col, val, meta)
    meta_ref[...] = meta


META_EXPERT, META_RANK, META_GATE = 0, 2, 4


def _route(x, mod, ng, w_router, b_router, d: Dims, plan):
    t_total, dm = x.shape
    tt = plan.tm
    ne = d.n_experts
    n_rt = t_total // tt
    tiles_per_seq = d.seq // tt
    wr = jnp.zeros((dm, V7X_LANES), F32).at[:, :ne].set(w_router)
    br = jnp.zeros((1, V7X_LANES), F32).at[0, :ne].set(b_router)
    ltri = jnp.tril(jnp.ones((tt, tt), BF16), -1)
    row = pl.BlockSpec((tt, dm), lambda i: (i, 0))
    meta = pl.BlockSpec((tt, V7X_LANES), lambda i: (i, 0))
    return pl.pallas_call(
        functools.partial(_router_body, n_experts=ne),
        grid=(n_rt,),
        in_specs=[row,
                  pl.BlockSpec((1, 6, dm), lambda i: (i // tiles_per_seq, 0, 0)),
                  _full((1, dm)), _full((dm, V7X_LANES)), _full((1, V7X_LANES)), _full((tt, tt))],
        out_specs=[row, meta, pl.BlockSpec((1, V7X_SUBLANES, V7X_LANES), lambda i: (i, 0, 0))],
        out_shape=[jax.ShapeDtypeStruct((t_total, dm), BF16),
                   jax.ShapeDtypeStruct((t_total, V7X_LANES), F32),
                   jax.ShapeDtypeStruct((n_rt, V7X_SUBLANES, V7X_LANES), F32)],
        compiler_params=_params("arbitrary"),
        name="moe_router",
    )(x, mod, ng[None], wr, br, ltri)


def _expert_key(meta, expert_lanes):
    key = jnp.full(expert_lanes.shape, -1.0, F32)
    for k in range(TOP_K):
        key = jnp.where(expert_lanes == meta[:, META_EXPERT + k:META_EXPERT + k + 1],
                        meta[:, META_RANK + k:META_RANK + k + 1], key)
    return key


def _chunks(count, ch):
    return lax.shift_right_logical(count + (ch - 1), jnp.int32(ch.bit_length() - 1))


def _dispatch_body(seg_ref, cnt_ref, zf_ref, nv_ref, h_ref, meta_ref, xs_hbm, stage, sem,
                   *, n_experts, ch, tg, n_fill_tiles):
    m = pl.program_id(0)
    tt = h_ref.shape[0]

    def block_copy(src_row, dst_row, rows):
        return pltpu.make_async_copy(stage.at[pl.ds(src_row, rows), :],
                                     xs_hbm.at[pl.ds(dst_row, rows), :], sem)

    @pl.when(m == 0)
    def _():
        stage[0:tg + ch, :] = jnp.zeros((tg + ch, stage.shape[1]), stage.dtype)
        for e in range(n_experts):
            cp = block_copy(0, pl.multiple_of(zf_ref[e], ROW_ALIGN), tg + ch)
            cp.start()
            cp.wait()

        def fill(j, c):
            cp = block_copy(0, pl.multiple_of(j * tg, tg), tg)
            cp.start()
            cp.wait()
            return c

        lax.fori_loop(nv_ref[0], n_fill_tiles, fill, 0)

    lanes = lax.broadcasted_iota(I32, (tt, V7X_LANES), 1).astype(F32)
    key_rows = jnp.transpose(_expert_key(meta_ref[...], lanes))
    hb = h_ref[...]
    cursor = jnp.int32(0)
    for e in range(n_experts):
        start = seg_ref[m * n_experts + e]
        key = key_rows[e:e + 1, :]

        def chunk(c, cur, start=start, key=key):
            cur = pl.multiple_of(cur, ch)
            slot = (lax.broadcasted_iota(I32, (ch, tt), 0) + c * ch).astype(F32)
            onehot = jnp.where(slot == key, 1.0, 0.0).astype(BF16)
            stage[pl.ds(cur, ch), :] = _dot(onehot, hb).astype(stage.dtype)
            block_copy(cur, pl.multiple_of(start + c * ch, ROW_ALIGN), ch).start()
            return cur + ch

        cursor = lax.fori_loop(0, _chunks(cnt_ref[m * n_experts + e], ch), chunk, cursor)

    def drain(i, c):
        block_copy(0, 0, ch).wait()
        return c

    lax.fori_loop(0, _chunks(cursor, ch), drain, 0)


def _dispatch(h2, meta, seg_start, seg_cnt, zfill_start, n_valid, n_rows, d: Dims, plan):
    t_total, dm = h2.shape
    tt, tg, ch = plan.tm, plan.tg, plan.chunk
    ne = d.n_experts
    grid_spec = pltpu.PrefetchScalarGridSpec(
        num_scalar_prefetch=4,
        grid=(t_total // tt,),
        in_specs=[pl.BlockSpec((tt, dm), lambda i, *_: (i, 0)),
                  pl.BlockSpec((tt, V7X_LANES), lambda i, *_: (i, 0))],
        out_specs=pl.BlockSpec(memory_space=pl.ANY),
        scratch_shapes=[pltpu.VMEM((max(TOP_K * tt + ne * ch, tg + ch), dm), BF16),
                        pltpu.SemaphoreType.DMA(())],
    )
    body = functools.partial(_dispatch_body, n_experts=ne, ch=ch, tg=tg, n_fill_tiles=n_rows // tg)
    return pl.pallas_call(
        body,
        grid_spec=grid_spec,
        out_shape=jax.ShapeDtypeStruct((n_rows, dm), BF16),
        compiler_params=_params("arbitrary"),
        name="moe_dispatch",
    )(seg_start, seg_cnt, zfill_start, n_valid, h2, meta)


def _combine_body(seg_ref, cnt_ref, ys_hbm, x_ref, mod_ref, meta_ref, o_ref, stage, acc_ref, sem,
                  *, n_experts, ch):
    m = pl.program_id(0)
    tt = x_ref.shape[0]

    def fetch(src_row, dst_row):
        return pltpu.make_async_copy(ys_hbm.at[pl.ds(src_row, ch), :],
                                     stage.at[pl.ds(dst_row, ch), :], sem)

    cursor = jnp.int32(0)
    for e in range(n_experts):
        start = seg_ref[m * n_experts + e]

        def issue(c, cur, start=start):
            fetch(pl.multiple_of(start + c * ch, ROW_ALIGN), pl.multiple_of(cur, ch)).start()
            return cur + ch

        cursor = lax.fori_loop(0, _chunks(cnt_ref[m * n_experts + e], ch), issue, cursor)

    def drain(i, c):
        fetch(0, 0).wait()
        return c

    lax.fori_loop(0, _chunks(cursor, ch), drain, 0)

    meta = meta_ref[...]
    acc_ref[...] = jnp.zeros_like(acc_ref)
    cursor = jnp.int32(0)
    for e in range(n_experts):
        key = jnp.full((tt, 1), -1.0, F32)
        gate = jnp.zeros((tt, 1), F32)
        for k in range(TOP_K):
            hit = meta[:, META_EXPERT + k:META_EXPERT + k + 1] == float(e)
            key = jnp.where(hit, meta[:, META_RANK + k:META_RANK + k + 1], key)
            gate = jnp.where(hit, meta[:, META_GATE + k:META_GATE + k + 1], gate)

        def chunk(c, cur, key=key, gate=gate):
            cur = pl.multiple_of(cur, ch)
            slot = (lax.broadcasted_iota(I32, (tt, ch), 1) + c * ch).astype(F32)
            onehot = jnp.where(slot == key, 1.0, 0.0).astype(BF16)
            acc_ref[...] += gate * _dot(onehot, stage[pl.ds(cur, ch), :])
            return cur + ch

        cursor = lax.fori_loop(0, _chunks(cnt_ref[m * n_experts + e], ch), chunk, cursor)

    o_ref[...] = x_ref[...] + mod_ref[0][5:6] * acc_ref[...]


def _combine(ys, seg_start, seg_cnt, x, mod, meta, d: Dims, plan):
    t_total, dm = x.shape
    tt, ch = plan.tm, plan.chunk
    ne = d.n_experts
    tiles_per_seq = d.seq // tt
    row = pl.BlockSpec((tt, dm), lambda i, *_: (i, 0))
    grid_spec = pltpu.PrefetchScalarGridSpec(
        num_scalar_prefetch=2,
        grid=(t_total // tt,),
        in_specs=[pl.BlockSpec(memory_space=pl.ANY), row,
                  pl.BlockSpec((1, 6, dm), lambda i, *_: (i // tiles_per_seq, 0, 0)),
                  pl.BlockSpec((tt, V7X_LANES), lambda i, *_: (i, 0))],
        out_specs=row,
        scratch_shapes=[pltpu.VMEM((TOP_K * tt + ne * ch, dm), BF16), pltpu.VMEM((tt, dm), F32),
                        pltpu.SemaphoreType.DMA(())],
    )
    return pl.pallas_call(
        functools.partial(_combine_body, n_experts=ne, ch=ch),
        grid_spec=grid_spec,
        out_shape=jax.ShapeDtypeStruct((t_total, dm), F32),
        compiler_params=_params("arbitrary"),
        name="moe_combine",
    )(seg_start, seg_cnt, ys, x, mod, meta)


def _round_up(v, q):
    return (v + q - 1) // q * q


def _moe_sublayer(x, mod, ng, w_router, b_router, wg, wu, wd, d: Dims, plan):
    t_total, dm = x.shape
    ne, tg, ch, tt = d.n_experts, plan.tg, plan.chunk, plan.tm
    n_rt = t_total // tt
    h2, meta, cnt = _route(x, mod, ng, w_router, b_router, d, plan)
    seg_cnt = cnt[:, 0, :ne].astype(I32)
    seg_len = _round_up(seg_cnt, ROW_ALIGN)
    used = jnp.sum(seg_len, axis=0)
    region = _round_up(used + ch, tg)
    region_end = jnp.cumsum(region)
    region_start = region_end - region
    seg_start = region_start[None, :] + jnp.cumsum(seg_len, axis=0) - seg_len
    n_valid = (region_end[ne - 1] // tg).astype(I32)
    n_tiles = -(-(TOP_K * t_total + n_rt * ne * (ROW_ALIGN - 1) + ne * ch) // tg) + ne
    tile_id = jnp.minimum(jnp.arange(n_tiles, dtype=I32), n_valid - 1)
    tile_expert = jnp.minimum(jnp.searchsorted(region_end // tg, tile_id, side="right"), ne - 1).astype(I32)
    seg_start = seg_start.reshape(-1).astype(I32)
    seg_cnt = seg_cnt.reshape(-1)
    xs = _dispatch(h2, meta, seg_start, seg_cnt, (region_start + used).astype(I32), n_valid.reshape(1),
                   (n_tiles + 2) * tg, d, plan)
    ys = _expert_swiglu(xs, tile_expert, n_valid.reshape(1), n_tiles, wg, wu, wd, plan)
    return _combine(ys, seg_start, seg_cnt, x, mod, meta, d, plan)


def kernel(x, c, positions, norm_g, w_ada, b_ada, conv_w_pw1, conv_b_pw1, conv_w_dw, conv_b_dw, conv_ln_g, conv_ln_b, conv_w_pw2, conv_b_pw2, attn_w_qkv, attn_q_gain, attn_k_gain, attn_sinks, attn_w_o, ffn_w_gate, ffn_w_up, ffn_w_down, moe_w_router, moe_b_router, moe_w_gate, moe_w_up, moe_w_down):
    bsz, seq, dm = x.shape
    head_dim = attn_q_gain.shape[1]
    n_heads = attn_sinks.shape[1]
    n_kv = (attn_w_qkv.shape[2] // head_dim - n_heads) // 2
    d = Dims(batch=bsz, seq=seq, d_model=dm, depth=norm_g.shape[0], conv_kernel=conv_w_dw.shape[1],
             n_heads=n_heads, n_kv_heads=n_kv, head_dim=head_dim, d_ff_dense=ffn_w_gate.shape[2],
             n_experts=moe_w_router.shape[2], d_ff_expert=moe_w_gate.shape[3])
    assert 2 * head_dim == V7X_LANES and (n_heads // n_kv) % 2 == 0 and n_heads % n_kv == 0
    assert seq % ATTN_BLOCK == 0 and dm % V7X_MXU_DIM == 0
    assert (n_heads + 2 * n_kv) * head_dim % V7X_MXU_DIM == 0 and n_kv * 2 * head_dim % V7X_MXU_DIM == 0
    plan = _make_plan(d)
    t_total = bsz * seq
    nq = n_heads * head_dim

    mod_all = _ada_all_layers(c, w_ada, b_ada, plan).reshape(d.depth, bsz, 6, dm)
    rope = _rope_tables(positions, d, plan)
    blk_id = jnp.arange(V7X_MXU_DIM) // head_dim
    head_block_ones = (blk_id[:, None] == blk_id[None, :]).astype(BF16)

    xt = x.reshape(t_total, dm)
    for i in range(d.depth):
        j = i // 2
        mod = mod_all[i]
        if i % 2 == 0:
            xt = _conv_sublayer(xt, mod, norm_g[i, 0], conv_w_pw1[j].astype(BF16), conv_b_pw1[j],
                                conv_w_dw[j], conv_b_dw[j], conv_ln_g[j], conv_ln_b[j],
                                conv_w_pw2[j].astype(BF16), conv_b_pw2[j], d, plan)
            xt = _ffn_sublayer(xt, mod, norm_g[i, 1], ffn_w_gate[j].astype(BF16),
                               ffn_w_up[j].astype(BF16), ffn_w_down[j].astype(BF16), d, plan)
        else:
            wqkv = attn_w_qkv[j]
            nkv = n_kv * head_dim
            dup = lambda w: jnp.repeat(w.reshape(dm, n_kv, 1, head_dim), 2, axis=2).reshape(dm, 2 * nkv)
            w_qkvd = jnp.concatenate([wqkv[:, :nq], dup(wqkv[:, nq:nq + nkv]), dup(wqkv[:, nq + nkv:])],
                                     axis=1).astype(BF16)
            gain_row = jnp.concatenate([jnp.tile(attn_q_gain[j] * head_dim ** -0.5, n_heads),
                                        jnp.tile(attn_k_gain[j], 2 * n_kv)])[None]
            q, kd, vd = _qkv_project(xt, mod, norm_g[i, 0], w_qkvd, gain_row, head_block_ones, rope, d, plan)
            xt = _attention(q, kd, vd, attn_sinks[j], xt, mod, attn_w_o[j].astype(BF16), d, plan)
            xt = _moe_sublayer(xt, mod, norm_g[i, 1], moe_w_router[j], moe_b_router[j],
                               moe_w_gate[j].astype(BF16), moe_w_up[j].astype(BF16),
                               moe_w_down[j].astype(BF16), d, plan)
    return xt.reshape(bsz, seq, dm)
```

```python
import functools
from typing import NamedTuple

import jax
import jax.numpy as jnp
from jax import lax
from jax.experimental import pallas as pl
from jax.experimental.pallas import tpu as pltpu

F32 = jnp.float32
BF16 = jnp.bfloat16
I32 = jnp.int32

V7X_LANES = 128
V7X_SUBLANES = 8
V7X_MXU_DIM = 256
ROW_ALIGN = 2 * V7X_SUBLANES
V7X_VMEM_LIMIT_BYTES = 56 * 1024 * 1024

ROPE_THETA = 500000.0
EPS = 1e-6
ATTN_BLOCK = 128
TOP_K = 2
MASK_VALUE = -1e30


class Dims(NamedTuple):
    batch: int
    seq: int
    d_model: int
    depth: int
    conv_kernel: int
    n_heads: int
    n_kv_heads: int
    head_dim: int
    d_ff_dense: int
    n_experts: int
    d_ff_expert: int


class Plan(NamedTuple):
    tm: int
    tf_expert: int
    tg: int
    expert_sub: int
    tn_ada: int
    conv_rows: int
    conv_cols: int
    chunk: int


def _largest_tile(total, target, quantum):
    best = None
    t = quantum
    while t <= min(total, target):
        if total % t == 0:
            best = t
        t += quantum
    return best if best is not None else total


def _make_plan(d: Dims) -> Plan:
    tm = _largest_tile(d.seq, 512, ATTN_BLOCK)
    return Plan(
        tm=tm,
        tf_expert=_largest_tile(d.d_ff_expert, 1792, V7X_MXU_DIM),
        tg=_largest_tile(d.batch * d.seq, 1024, ATTN_BLOCK),
        expert_sub=2,
        tn_ada=_largest_tile(6 * d.d_model, 1536, V7X_LANES),
        conv_rows=64,
        conv_cols=V7X_LANES,
        chunk=ATTN_BLOCK,
    )


def _params(*semantics):
    return pltpu.CompilerParams(dimension_semantics=semantics,
                                vmem_limit_bytes=V7X_VMEM_LIMIT_BYTES)


def _dot(a, b):
    return jnp.dot(a, b, preferred_element_type=F32)


def _sigmoid(x):
    return 1.0 / (1.0 + jnp.exp(-x))


def _norm_mod(x, g_row, shift_row, scale_row):
    y = x * lax.rsqrt(jnp.mean(x * x, axis=-1, keepdims=True) + EPS)
    return (y * g_row) * (1.0 + scale_row) + shift_row


def _full(shape):
    return pl.BlockSpec(shape, lambda *_: (0,) * len(shape))


def _ada_body(c_ref, w_ref, b_ref, o_ref):
    c = c_ref[...]
    ca = c * _sigmoid(c)
    o_ref[0] = jnp.dot(ca, w_ref[0], preferred_element_type=F32,
                       precision=lax.Precision.HIGHEST) + b_ref[0]


def _ada_all_layers(c, w_ada, b_ada, plan):
    depth, dm, n6 = w_ada.shape
    bsz = c.shape[0]
    tn = plan.tn_ada
    return pl.pallas_call(
        _ada_body,
        grid=(depth, n6 // tn),
        in_specs=[_full((bsz, dm)),
                  pl.BlockSpec((1, dm, tn), lambda i, j: (i, 0, j)),
                  pl.BlockSpec((1, 1, tn), lambda i, j: (i, 0, j))],
        out_specs=pl.BlockSpec((1, bsz, tn), lambda i, j: (i, 0, j)),
        out_shape=jax.ShapeDtypeStruct((depth, bsz, n6), F32),
        compiler_params=_params("arbitrary", "arbitrary"),
        name="adaln_mod",
    )(c, w_ada, b_ada.reshape(depth, 1, n6))


def _rope_body(pos_ref, f_ref, m1_ref, m2_ref, c_ref, s1_ref, s2_ref):
    ang = pos_ref[...].astype(F32) * f_ref[...]
    s = jnp.sin(ang)
    c_ref[...] = jnp.cos(ang)
    s1_ref[...] = s * m1_ref[...]
    s2_ref[...] = s * m2_ref[...]


def _rope_tables(positions, d: Dims, plan):
    t_total = d.batch * d.seq
    rot = d.head_dim // 4
    half = rot // 2
    inv_freq = ROPE_THETA ** (-jnp.arange(0, rot, 2, dtype=F32) / rot)
    lane = jnp.arange(V7X_LANES)
    in_head = lane % d.head_dim
    freq_row = jnp.where(in_head < rot, inv_freq[in_head % half], 0.0).astype(F32)[None]
    m1 = jnp.where((in_head >= half) & (in_head < rot), 1.0, 0.0).astype(F32)[None]
    m2 = jnp.where(in_head < half, -1.0, 0.0).astype(F32)[None]
    tm = plan.tm
    row = pl.BlockSpec((tm, V7X_LANES), lambda i: (i, 0))
    tab = jax.ShapeDtypeStruct((t_total, V7X_LANES), F32)
    return pl.pallas_call(
        _rope_body,
        grid=(t_total // tm,),
        in_specs=[pl.BlockSpec((tm, 1), lambda i: (i, 0)),
                  _full((1, V7X_LANES)), _full((1, V7X_LANES)), _full((1, V7X_LANES))],
        out_specs=[row, row, row],
        out_shape=[tab, tab, tab],
        compiler_params=_params("arbitrary"),
        name="rope_tables",
    )(positions.reshape(t_total, 1), freq_row, m1, m2)


def _conv_body(x_ref, mod_ref, ng_ref, w1_ref, b1_ref, wd_ref, bd_ref, lg_ref, lb_ref,
               w2_ref, b2_ref, o_ref, ubuf, vbuf, *, dm, ktaps, halo, tiles_per_seq, rc, cb):
    m = pl.program_id(0)
    tm = x_ref.shape[0]
    x = x_ref[...]
    mod = mod_ref[0]
    h = _norm_mod(x, ng_ref[...], mod[0:1], mod[1:2]).astype(BF16)
    u = _dot(h, w1_ref[...]) + b1_ref[...]
    glu = u[:, :dm] * _sigmoid(u[:, dm:])

    @pl.when(m % tiles_per_seq == 0)
    def _():
        ubuf[0:halo, :] = jnp.zeros((halo, dm), F32)

    ubuf[halo:halo + tm, :] = glu

    def chunk(ci, carry):
        r0 = pl.multiple_of(ci * rc, rc)
        pieces = []
        for cbi in range(dm // cb):
            cols = slice(cbi * cb, (cbi + 1) * cb)
            win = ubuf[pl.ds(r0, rc + halo), cols]
            acc = None
            for r in range(V7X_SUBLANES):
                z = None
                for q in range(halo // V7X_SUBLANES):
                    dl = V7X_SUBLANES * q + r
                    if dl >= ktaps:
                        continue
                    s = halo - V7X_SUBLANES * (q + 1)
                    term = wd_ref[dl:dl + 1, cols] * win[s:s + rc + V7X_SUBLANES, :]
                    z = term if z is None else z + term
                if z is None:
                    continue
                if r:
                    z = pltpu.roll(z, r, 0)
                z = z[V7X_SUBLANES:V7X_SUBLANES + rc, :]
                acc = z if acc is None else acc + z
            pieces.append(acc)
        conv = jnp.concatenate(pieces, axis=1) + bd_ref[...]
        mu = jnp.mean(conv, axis=-1, keepdims=True)
        cen = conv - mu
        var = jnp.mean(cen * cen, axis=-1, keepdims=True)
        y = cen * lax.rsqrt(var + EPS) * lg_ref[...] + lb_ref[...]
        vbuf[pl.ds(r0, rc), :] = (y * _sigmoid(y)).astype(BF16)
        return carry

    lax.fori_loop(0, tm // rc, chunk, 0)
    ubuf[0:halo, :] = ubuf[tm:tm + halo, :]
    y2 = _dot(vbuf[...], w2_ref[...]) + b2_ref[...]
    o_ref[...] = x + mod[2:3] * y2


def _conv_sublayer(x, mod, ng, w1, b1, w_dw, b_dw, ln_g, ln_b, w2, b2, d: Dims, plan):
    t_total, dm = x.shape
    tm = plan.tm
    ktaps = w_dw.shape[0]
    halo = -(-ktaps // V7X_SUBLANES) * V7X_SUBLANES
    wd = jnp.zeros((halo, dm), F32).at[:ktaps].set(w_dw[::-1])
    tiles_per_seq = d.seq // tm
    row = pl.BlockSpec((tm, dm), lambda i: (i, 0))
    body = functools.partial(_conv_body, dm=dm, ktaps=ktaps, halo=halo,
                             tiles_per_seq=tiles_per_seq, rc=plan.conv_rows, cb=plan.conv_cols)
    return pl.pallas_call(
        body,
        grid=(t_total // tm,),
        in_specs=[row,
                  pl.BlockSpec((1, 6, dm), lambda i: (i // tiles_per_seq, 0, 0)),
                  _full((1, dm)), _full((dm, 2 * dm)), _full((1, 2 * dm)),
                  _full(wd.shape), _full((1, dm)), _full((1, dm)), _full((1, dm)),
                  _full((dm, dm)), _full((1, dm))],
        out_specs=row,
        out_shape=jax.ShapeDtypeStruct((t_total, dm), F32),
        scratch_shapes=[pltpu.VMEM((halo + tm, dm), F32), pltpu.VMEM((tm, dm), BF16)],
        compiler_params=_params("arbitrary"),
        name="conv_sublayer",
    )(x, mod, ng[None], w1, b1[None], wd, b_dw[None], ln_g[None], ln_b[None], w2, b2[None])


def _swiglu(h, wg, wu, wd):
    g = _dot(h, wg)
    u = _dot(h, wu)
    return _dot(((g * _sigmoid(g)) * u).astype(BF16), wd)


def _ffn_body(x_ref, mod_ref, ng_ref, wg_ref, wu_ref, wd_ref, o_ref):
    mod = mod_ref[0]
    x = x_ref[...]
    h = _norm_mod(x, ng_ref[...], mod[3:4], mod[4:5]).astype(BF16)
    o_ref[...] = x + mod[5:6] * _swiglu(h, wg_ref[...], wu_ref[...], wd_ref[...])


def _ffn_sublayer(x, mod, ng, wg, wu, wd, d: Dims, plan):
    t_total, dm = x.shape
    tm = plan.tm
    tiles_per_seq = d.seq // tm
    row = pl.BlockSpec((tm, dm), lambda i: (i, 0))
    resident = pl.BlockSpec(memory_space=pltpu.VMEM)
    return pl.pallas_call(
        _ffn_body,
        grid=(t_total // tm,),
        in_specs=[row,
                  pl.BlockSpec((1, 6, dm), lambda i: (i // tiles_per_seq, 0, 0)),
                  _full((1, dm)), resident, resident, resident],
        out_specs=row,
        out_shape=jax.ShapeDtypeStruct((t_total, dm), F32),
        compiler_params=_params("arbitrary"),
        name="dense_swiglu",
    )(x, mod, ng[None], wg, wu, wd)


def _expert_body(te_ref, nv_ref, xs_ref, wg_ref, wu_ref, wd_ref, ys_ref, acc_ref, *, n_sub):
    i = pl.program_id(0)
    f = pl.program_id(1)
    rows = xs_ref.shape[0] // n_sub

    @pl.when(i < nv_ref[0])
    def _():
        @pl.when(f == 0)
        def _():
            acc_ref[...] = jnp.zeros_like(acc_ref)

        for s in range(n_sub):
            sl = slice(s * rows, (s + 1) * rows)
            acc_ref[sl, :] += _swiglu(xs_ref[sl, :], wg_ref[0], wu_ref[0], wd_ref[0])

        @pl.when(f == pl.num_programs(1) - 1)
        def _():
            ys_ref[...] = acc_ref[...].astype(ys_ref.dtype)

    @pl.when((i >= nv_ref[0]) & (f == 0))
    def _():
        ys_ref[...] = jnp.zeros_like(ys_ref)


def _expert_swiglu(xs, tile_expert, n_valid, n_tiles, wg, wu, wd, plan):
    dm = xs.shape[1]
    ff = wg.shape[2]
    tg, tf = plan.tg, plan.tf_expert
    p_rows = n_tiles * tg
    nf = ff // tf

    def row_map(i, f, te, nv):
        return (jnp.minimum(i, nv[0] - 1), 0)

    def fsel(i, f, nv):
        return jnp.where(i < nv[0], f, nf - 1)

    grid_spec = pltpu.PrefetchScalarGridSpec(
        num_scalar_prefetch=2,
        grid=(p_rows // tg, nf),
        in_specs=[pl.BlockSpec((tg, dm), row_map),
                  pl.BlockSpec((1, dm, tf), lambda i, f, te, nv: (te[i], 0, fsel(i, f, nv))),
                  pl.BlockSpec((1, dm, tf), lambda i, f, te, nv: (te[i], 0, fsel(i, f, nv))),
                  pl.BlockSpec((1, tf, dm), lambda i, f, te, nv: (te[i], fsel(i, f, nv), 0))],
        out_specs=pl.BlockSpec((tg, dm), lambda i, f, te, nv: (i, 0)),
        scratch_shapes=[pltpu.VMEM((tg, dm), F32)],
    )
    return pl.pallas_call(
        functools.partial(_expert_body, n_sub=plan.expert_sub),
        grid_spec=grid_spec,
        out_shape=jax.ShapeDtypeStruct((p_rows, dm), BF16),
        compiler_params=_params("arbitrary", "arbitrary"),
        name="expert_swiglu",
    )(tile_expert, n_valid, xs, wg, wu, wd)


def _qkv_body(x_ref, mod_ref, ng_ref, w_ref, gain_ref, bd_ref, c_ref, s1_ref, s2_ref,
              q_ref, k_ref, v_ref, *, nq, nk, head_dim, half_rot):
    mod = mod_ref[0]
    h = _norm_mod(x_ref[...], ng_ref[...], mod[0:1], mod[1:2]).astype(BF16)
    qkv = _dot(h, w_ref[...])
    v_ref[...] = qkv[:, nq + nk:].astype(BF16)
    cos = c_ref[...]
    sin_hi = s1_ref[...]
    sin_lo = s2_ref[...]
    bd = bd_ref[...]
    for j in range((nq + nk) // V7X_MXU_DIM):
        blk = qkv[:, j * V7X_MXU_DIM:(j + 1) * V7X_MXU_DIM]
        ss = _dot((blk * blk).astype(BF16), bd)
        yn = (blk * lax.rsqrt(ss * (1.0 / head_dim) + EPS)) * gain_ref[:, j * V7X_MXU_DIM:(j + 1) * V7X_MXU_DIM]
        for p in range(V7X_MXU_DIM // V7X_LANES):
            xx = yn[:, p * V7X_LANES:(p + 1) * V7X_LANES]
            rot = (xx * cos + pltpu.roll(xx, half_rot, 1) * sin_hi
                   + pltpu.roll(xx, V7X_LANES - half_rot, 1) * sin_lo).astype(BF16)
            col = j * V7X_MXU_DIM + p * V7X_LANES
            if col < nq:
                q_ref[:, col:col + V7X_LANES] = rot
            else:
                k_ref[:, col - nq:col - nq + V7X_LANES] = rot


def _qkv_project(x, mod, ng, w_qkvd, gain_row, bd, rope, d: Dims, plan):
    t_total, dm = x.shape
    tm = plan.tm
    nq = d.n_heads * d.head_dim
    nk = d.n_kv_heads * 2 * d.head_dim
    tiles_per_seq = d.seq // tm
    row = pl.BlockSpec((tm, dm), lambda i: (i, 0))
    tab = pl.BlockSpec((tm, V7X_LANES), lambda i: (i, 0))
    body = functools.partial(_qkv_body, nq=nq, nk=nk, head_dim=d.head_dim,
                             half_rot=d.head_dim // 8)
    return pl.pallas_call(
        body,
        grid=(t_total // tm,),
        in_specs=[row,
                  pl.BlockSpec((1, 6, dm), lambda i: (i // tiles_per_seq, 0, 0)),
                  _full((1, dm)), _full((dm, nq + 2 * nk)), _full((1, nq + nk)),
                  _full((V7X_MXU_DIM, V7X_MXU_DIM)), tab, tab, tab],
        out_specs=[pl.BlockSpec((tm, nq), lambda i: (i, 0)),
                   pl.BlockSpec((tm, nk), lambda i: (i, 0)),
                   pl.BlockSpec((tm, nk), lambda i: (i, 0))],
        out_shape=[jax.ShapeDtypeStruct((t_total, nq), BF16),
                   jax.ShapeDtypeStruct((t_total, nk), BF16),
                   jax.ShapeDtypeStruct((t_total, nk), BF16)],
        compiler_params=_params("arbitrary"),
        name="qkv_project",
    )(x, mod, ng[None], w_qkvd, gain_row, bd, *rope)


def _attn_body(q_ref, k_ref, v_ref, kp_ref, vp_ref, sink_ref, x_ref, mod_ref, wo_ref, o_ref,
               o_scr, *, n_kv, group, head_dim, tiles_per_seq):
    m = pl.program_id(0)
    tq = q_ref.shape[0]
    blk = ATTN_BLOCK
    qi = lax.broadcasted_iota(I32, (blk, 2 * blk), 0)
    kj = lax.broadcasted_iota(I32, (blk, 2 * blk), 1)
    diff = qi + blk - kj
    in_win = (diff >= 0) & (diff < blk)
    kmin = jnp.where(m % tiles_per_seq == 0, blk, 0)
    lo = lax.broadcasted_iota(I32, (blk, V7X_LANES), 1) < head_dim
    zero = jnp.zeros((blk, V7X_LANES), BF16)
    pairs = group // 2
    for n in range(tq // blk):
        rows = slice(n * blk, (n + 1) * blk)
        if n == 0:
            valid = in_win & (kj >= kmin)
            kprev, vprev = kp_ref[...], vp_ref[...]
        else:
            valid = in_win
            prev = slice((n - 1) * blk, n * blk)
            kprev, vprev = k_ref[prev, :], v_ref[prev, :]
        kcur, vcur = k_ref[rows, :], v_ref[rows, :]
        for g in range(n_kv):
            gl = slice(g * V7X_LANES, (g + 1) * V7X_LANES)
            kk = jnp.concatenate([kprev[:, gl], kcur[:, gl]], axis=0)
            vv = jnp.concatenate([vprev[:, gl], vcur[:, gl]], axis=0)
            slabs = []
            for pr in range(pairs):
                c0 = (g * pairs + pr) * V7X_LANES
                qp = q_ref[rows, c0:c0 + V7X_LANES]
                slabs += [jnp.where(lo, qp, zero), jnp.where(lo, zero, qp)]
            qs = jnp.concatenate(slabs, axis=0)
            s = lax.dot_general(qs, kk, (((1,), (1,)), ((), ())), preferred_element_type=F32)
            ps, dens = [], []
            for hh in range(group):
                sh = jnp.where(valid, s[hh * blk:(hh + 1) * blk, :], MASK_VALUE)
                sink = sink_ref[0, g * group + hh]
                mx = jnp.maximum(jnp.max(sh, axis=-1, keepdims=True), sink)
                p = jnp.exp(sh - mx)
                dens.append(jnp.sum(p, axis=-1, keepdims=True) + jnp.exp(sink - mx))
                ps.append(p.astype(BF16))
            o = _dot(jnp.concatenate(ps, axis=0), vv)
            for pr in range(pairs):
                h0, h1 = 2 * pr, 2 * pr + 1
                oa = o[h0 * blk:(h0 + 1) * blk, :] / dens[h0]
                ob = o[h1 * blk:(h1 + 1) * blk, :] / dens[h1]
                c0 = (g * pairs + pr) * V7X_LANES
                o_scr[rows, c0:c0 + V7X_LANES] = jnp.where(lo, oa, ob).astype(BF16)
    mod = mod_ref[0]
    o_ref[...] = x_ref[...] + mod[2:3] * _dot(o_scr[...], wo_ref[...])


def _attention(q, kd, vd, sinks, x, mod, wo, d: Dims, plan):
    t_total, dm = x.shape
    tq = plan.tm
    nq = q.shape[1]
    nk = kd.shape[1]
    tiles_per_seq = d.seq // tq
    bpt = tq // ATTN_BLOCK
    group = d.n_heads // d.n_kv_heads
    prev = pl.BlockSpec((ATTN_BLOCK, nk), lambda i: (jnp.maximum(i * bpt - 1, 0), 0))
    cur = pl.BlockSpec((tq, nk), lambda i: (i, 0))
    body = functools.partial(_attn_body, n_kv=d.n_kv_heads, group=group, head_dim=d.head_dim,
                             tiles_per_seq=tiles_per_seq)
    return pl.pallas_call(
        body,
        grid=(t_total // tq,),
        in_specs=[pl.BlockSpec((tq, nq), lambda i: (i, 0)), cur, cur, prev, prev,
                  pl.BlockSpec(memory_space=pltpu.SMEM),
                  pl.BlockSpec((tq, dm), lambda i: (i, 0)),
                  pl.BlockSpec((1, 6, dm), lambda i: (i // tiles_per_seq, 0, 0)),
                  _full((nq, dm))],
        out_specs=pl.BlockSpec((tq, dm), lambda i: (i, 0)),
        out_shape=jax.ShapeDtypeStruct((t_total, dm), F32),
        scratch_shapes=[pltpu.VMEM((tq, nq), BF16)],
        compiler_params=_params("arbitrary"),
        name="swa_attention",
    )(q, kd, vd, kd, vd, sinks[None], x, mod, wo)


def _router_body(x_ref, mod_ref, ng_ref, wr_ref, br_ref, ltri_ref, h_ref, meta_ref, cnt_ref,
                 *, n_experts):
    tt = x_ref.shape[0]
    mod = mod_ref[0]
    h2 = _norm_mod(x_ref[...], ng_ref[...], mod[3:4], mod[4:5])
    h_ref[...] = h2.astype(BF16)
    logits = jnp.dot(h2, wr_ref[...], preferred_element_type=F32,
                     precision=lax.Precision.HIGHEST) + br_ref[...]
    lane = lax.broadcasted_iota(I32, (tt, V7X_LANES), 1).astype(F32)
    neg = jnp.float32(-jnp.inf)
    logits = jnp.where(lane < n_experts, logits, neg)
    big = jnp.float32(V7X_LANES)
    m1 = jnp.max(logits, axis=-1, keepdims=True)
    i1 = jnp.min(jnp.where(logits == m1, lane, big), axis=-1, keepdims=True)
    rest = jnp.where(lane == i1, neg, logits)
    m2 = jnp.max(rest, axis=-1, keepdims=True)
    i2 = jnp.min(jnp.where(rest == m2, lane, big), axis=-1, keepdims=True)
    e2 = jnp.exp(m2 - m1)
    w1 = 1.0 / (1.0 + e2)
    w2 = e2 / (1.0 + e2)

    sel1 = lane == i1
    sel2 = lane == i2
    assigned = jnp.where(sel1 | sel2, 1.0, 0.0)
    before = _dot(ltri_ref[...], assigned.astype(BF16))
    r1 = jnp.sum(jnp.where(sel1, before, 0.0), axis=-1, keepdims=True)
    r2 = jnp.sum(jnp.where(sel2, before, 0.0), axis=-1, keepdims=True)
    cnt_ref[0] = jnp.broadcast_to(jnp.sum(assigned, axis=0, keepdims=True), cnt_ref.shape[1:])
    meta = jnp.zeros((tt, V7X_LANES), F32)
    for col, val in enumerate((i1, i2, r1, r2, w1, w2)):
        meta = jnp.where(lane == col, val, meta)
    meta_ref[...] = meta


META_EXPERT, META_RANK, META_GATE = 0, 2, 4


def _route(x, mod, ng, w_router, b_router, d: Dims, plan):
    t_total, dm = x.shape
    tt = plan.tm
    ne = d.n_experts
    n_rt = t_total // tt
    tiles_per_seq = d.seq // tt
    wr = jnp.zeros((dm, V7X_LANES), F32).at[:, :ne].set(w_router)
    br = jnp.zeros((1, V7X_LANES), F32).at[0, :ne].set(b_router)
    ltri = jnp.tril(jnp.ones((tt, tt), BF16), -1)
    row = pl.BlockSpec((tt, dm), lambda i: (i, 0))
    meta = pl.BlockSpec((tt, V7X_LANES), lambda i: (i, 0))
    return pl.pallas_call(
        functools.partial(_router_body, n_experts=ne),
        grid=(n_rt,),
        in_specs=[row,
                  pl.BlockSpec((1, 6, dm), lambda i: (i // tiles_per_seq, 0, 0)),
                  _full((1, dm)), _full((dm, V7X_LANES)), _full((1, V7X_LANES)), _full((tt, tt))],
        out_specs=[row, meta, pl.BlockSpec((1, V7X_SUBLANES, V7X_LANES), lambda i: (i, 0, 0))],
        out_shape=[jax.ShapeDtypeStruct((t_total, dm), BF16),
                   jax.ShapeDtypeStruct((t_total, V7X_LANES), F32),
                   jax.ShapeDtypeStruct((n_rt, V7X_SUBLANES, V7X_LANES), F32)],
        compiler_params=_params("arbitrary"),
        name="moe_router",
    )(x, mod, ng[None], wr, br, ltri)


def _chunks(count, ch):
    return lax.shift_right_logical(count + (ch - 1), jnp.int32(ch.bit_length() - 1))


def _stage_offsets(cnt_ref, base, n_experts, ch):
    offs = []
    total = jnp.int32(0)
    for e in range(n_experts):
        offs.append(total)
        total = total + _chunks(cnt_ref[base + e], ch) * ch
    return offs, total


def _stage_pos(meta, lanes, offs):
    off_row = jnp.zeros((1, lanes.shape[1]), F32)
    for e, off in enumerate(offs):
        off_row = jnp.where(lanes[0:1, :] == e, off.astype(F32), off_row)
    pos = []
    for k in range(TOP_K):
        expert = meta[:, META_EXPERT + k:META_EXPERT + k + 1]
        seg_off = jnp.sum(jnp.where(lanes == expert, off_row, 0.0), axis=-1, keepdims=True)
        pos.append(seg_off + meta[:, META_RANK + k:META_RANK + k + 1])
    return pos


def _dispatch_body(seg_ref, cnt_ref, zf_ref, nv_ref, h_ref, meta_ref, xs_hbm, stage, sem,
                   *, n_experts, ch, tg, n_fill_tiles):
    m = pl.program_id(0)
    tt = h_ref.shape[0]

    def block_copy(src_row, dst_row, rows):
        return pltpu.make_async_copy(stage.at[pl.ds(src_row, rows), :],
                                     xs_hbm.at[pl.ds(dst_row, rows), :], sem)

    @pl.when(m == 0)
    def _():
        stage[0:tg + ch, :] = jnp.zeros((tg + ch, stage.shape[1]), stage.dtype)
        for e in range(n_experts):
            cp = block_copy(0, pl.multiple_of(zf_ref[e], ROW_ALIGN), tg + ch)
            cp.start()
            cp.wait()

        def fill(j, c):
            cp = block_copy(0, pl.multiple_of(j * tg, tg), tg)
            cp.start()
            cp.wait()
            return c

        lax.fori_loop(nv_ref[0], n_fill_tiles, fill, 0)

    offs, total = _stage_offsets(cnt_ref, m * n_experts, n_experts, ch)
    lanes = lax.broadcasted_iota(I32, (tt, V7X_LANES), 1).astype(F32)
    pos = _stage_pos(meta_ref[...], lanes, offs)
    cols = jnp.full((tt, V7X_LANES), -1.0, F32)
    for k in range(TOP_K):
        cols = jnp.where(lanes == k, pos[k], cols)
    pos_rows = jnp.transpose(cols)
    slot = lax.broadcasted_iota(I32, (stage.shape[0], tt), 0).astype(F32)
    onehot = jnp.zeros(slot.shape, F32)
    for k in range(TOP_K):
        onehot = jnp.where(slot == pos_rows[k:k + 1, :], 1.0, onehot)
    stage[...] = _dot(onehot.astype(BF16), h_ref[...]).astype(stage.dtype)

    for e in range(n_experts):
        start = seg_ref[m * n_experts + e]

        def send(c, carry, start=start, off=offs[e]):
            block_copy(pl.multiple_of(off + c * ch, ch),
                       pl.multiple_of(start + c * ch, ROW_ALIGN), ch).start()
            return carry

        lax.fori_loop(0, _chunks(cnt_ref[m * n_experts + e], ch), send, 0)

    def drain(i, c):
        block_copy(0, 0, ch).wait()
        return c

    lax.fori_loop(0, _chunks(total, ch), drain, 0)


def _dispatch(h2, meta, seg_start, seg_cnt, zfill_start, n_valid, n_rows, d: Dims, plan):
    t_total, dm = h2.shape
    tt, tg, ch = plan.tm, plan.tg, plan.chunk
    ne = d.n_experts
    grid_spec = pltpu.PrefetchScalarGridSpec(
        num_scalar_prefetch=4,
        grid=(t_total // tt,),
        in_specs=[pl.BlockSpec((tt, dm), lambda i, *_: (i, 0)),
                  pl.BlockSpec((tt, V7X_LANES), lambda i, *_: (i, 0))],
        out_specs=pl.BlockSpec(memory_space=pl.ANY),
        scratch_shapes=[pltpu.VMEM((max(TOP_K * tt + ne * ch, tg + ch), dm), BF16),
                        pltpu.SemaphoreType.DMA(())],
    )
    body = functools.partial(_dispatch_body, n_experts=ne, ch=ch, tg=tg, n_fill_tiles=n_rows // tg)
    return pl.pallas_call(
        body,
        grid_spec=grid_spec,
        out_shape=jax.ShapeDtypeStruct((n_rows, dm), BF16),
        compiler_params=_params("arbitrary"),
        name="moe_dispatch",
    )(seg_start, seg_cnt, zfill_start, n_valid, h2, meta)


def _combine_body(seg_ref, cnt_ref, ys_hbm, x_ref, mod_ref, meta_ref, o_ref, stage, sem,
                  *, n_experts, ch):
    m = pl.program_id(0)
    tt = x_ref.shape[0]

    def fetch(src_row, dst_row):
        return pltpu.make_async_copy(ys_hbm.at[pl.ds(src_row, ch), :],
                                     stage.at[pl.ds(dst_row, ch), :], sem)

    @pl.when(m == 0)
    def _():
        stage[...] = jnp.zeros_like(stage)

    offs, total = _stage_offsets(cnt_ref, m * n_experts, n_experts, ch)
    for e in range(n_experts):
        start = seg_ref[m * n_experts + e]

        def issue(c, carry, start=start, off=offs[e]):
            fetch(pl.multiple_of(start + c * ch, ROW_ALIGN), pl.multiple_of(off + c * ch, ch)).start()
            return carry

        lax.fori_loop(0, _chunks(cnt_ref[m * n_experts + e], ch), issue, 0)

    meta = meta_ref[...]
    lanes = lax.broadcasted_iota(I32, (tt, V7X_LANES), 1).astype(F32)
    pos = _stage_pos(meta, lanes, offs)
    slot = lax.broadcasted_iota(I32, (tt, stage.shape[0]), 1).astype(F32)
    weights = jnp.zeros(slot.shape, F32)
    for k in range(TOP_K):
        weights = jnp.where(slot == pos[k], meta[:, META_GATE + k:META_GATE + k + 1], weights)

    def drain(i, c):
        fetch(0, 0).wait()
        return c

    lax.fori_loop(0, _chunks(total, ch), drain, 0)
    y = _dot(weights.astype(BF16), stage[...])
    o_ref[...] = x_ref[...] + mod_ref[0][5:6] * y


def _combine(ys, seg_start, seg_cnt, x, mod, meta, d: Dims, plan):
    t_total, dm = x.shape
    tt, ch = plan.tm, plan.chunk
    ne = d.n_experts
    tiles_per_seq = d.seq // tt
    row = pl.BlockSpec((tt, dm), lambda i, *_: (i, 0))
    grid_spec = pltpu.PrefetchScalarGridSpec(
        num_scalar_prefetch=2,
        grid=(t_total // tt,),
        in_specs=[pl.BlockSpec(memory_space=pl.ANY), row,
                  pl.BlockSpec((1, 6, dm), lambda i, *_: (i // tiles_per_seq, 0, 0)),
                  pl.BlockSpec((tt, V7X_LANES), lambda i, *_: (i, 0))],
        out_specs=row,
        scratch_shapes=[pltpu.VMEM((TOP_K * tt + ne * ch, dm), BF16), pltpu.SemaphoreType.DMA(())],
    )
    return pl.pallas_call(
        functools.partial(_combine_body, n_experts=ne, ch=ch),
        grid_spec=grid_spec,
        out_shape=jax.ShapeDtypeStruct((t_total, dm), F32),
        compiler_params=_params("arbitrary"),
        name="moe_combine",
    )(seg_start, seg_cnt, ys, x, mod, meta)


def _round_up(v, q):
    return (v + q - 1) // q * q


def _moe_sublayer(x, mod, ng, w_router, b_router, wg, wu, wd, layer, d: Dims, plan):
    t_total, dm = x.shape
    ne, tg, ch, tt = d.n_experts, plan.tg, plan.chunk, plan.tm
    n_rt = t_total // tt
    h2, meta, cnt = _route(x, mod, ng, w_router, b_router, d, plan)
    seg_cnt = cnt[:, 0, :ne].astype(I32)
    seg_len = _round_up(seg_cnt, ROW_ALIGN)
    used = jnp.sum(seg_len, axis=0)
    region = _round_up(used + ch, tg)
    region_end = jnp.cumsum(region)
    region_start = region_end - region
    seg_start = region_start[None, :] + jnp.cumsum(seg_len, axis=0) - seg_len
    n_valid = (region_end[ne - 1] // tg).astype(I32)
    n_tiles = -(-(TOP_K * t_total + n_rt * ne * (ROW_ALIGN - 1) + ne * ch) // tg) + ne
    tile_id = jnp.minimum(jnp.arange(n_tiles, dtype=I32), n_valid - 1)
    tile_expert = jnp.sum(tile_id[:, None] >= (region_end // tg)[None, :ne - 1], axis=1).astype(I32)
    tile_expert = tile_expert + layer * ne
    seg_start = seg_start.reshape(-1).astype(I32)
    seg_cnt = seg_cnt.reshape(-1)
    xs = _dispatch(h2, meta, seg_start, seg_cnt, (region_start + used).astype(I32), n_valid.reshape(1),
                   (n_tiles + 2) * tg, d, plan)
    ys = _expert_swiglu(xs, tile_expert, n_valid.reshape(1), n_tiles, wg, wu, wd, plan)
    return _combine(ys, seg_start, seg_cnt, x, mod, meta, d, plan)


def kernel(x, c, positions, norm_g, w_ada, b_ada, conv_w_pw1, conv_b_pw1, conv_w_dw, conv_b_dw, conv_ln_g, conv_ln_b, conv_w_pw2, conv_b_pw2, attn_w_qkv, attn_q_gain, attn_k_gain, attn_sinks, attn_w_o, ffn_w_gate, ffn_w_up, ffn_w_down, moe_w_router, moe_b_router, moe_w_gate, moe_w_up, moe_w_down):
    bsz, seq, dm = x.shape
    head_dim = attn_q_gain.shape[1]
    n_heads = attn_sinks.shape[1]
    n_kv = (attn_w_qkv.shape[2] // head_dim - n_heads) // 2
    d = Dims(batch=bsz, seq=seq, d_model=dm, depth=norm_g.shape[0], conv_kernel=conv_w_dw.shape[1],
             n_heads=n_heads, n_kv_heads=n_kv, head_dim=head_dim, d_ff_dense=ffn_w_gate.shape[2],
             n_experts=moe_w_router.shape[2], d_ff_expert=moe_w_gate.shape[3])
    assert 2 * head_dim == V7X_LANES and (n_heads // n_kv) % 2 == 0 and n_heads % n_kv == 0
    assert seq % ATTN_BLOCK == 0 and dm % V7X_MXU_DIM == 0
    assert (n_heads + 2 * n_kv) * head_dim % V7X_MXU_DIM == 0 and n_kv * 2 * head_dim % V7X_MXU_DIM == 0
    plan = _make_plan(d)
    t_total = bsz * seq
    nq = n_heads * head_dim

    mod_all = _ada_all_layers(c, w_ada, b_ada, plan).reshape(d.depth, bsz, 6, dm)
    rope = _rope_tables(positions, d, plan)
    blk_id = jnp.arange(V7X_MXU_DIM) // head_dim
    head_block_ones = (blk_id[:, None] == blk_id[None, :]).astype(BF16)
    moe_wg = moe_w_gate.astype(BF16).reshape((-1,) + moe_w_gate.shape[2:])
    moe_wu = moe_w_up.astype(BF16).reshape((-1,) + moe_w_up.shape[2:])
    moe_wd = moe_w_down.astype(BF16).reshape((-1,) + moe_w_down.shape[2:])

    xt = x.reshape(t_total, dm)
    for i in range(d.depth):
        j = i // 2
        mod = mod_all[i]
        if i % 2 == 0:
            xt = _conv_sublayer(xt, mod, norm_g[i, 0], conv_w_pw1[j].astype(BF16), conv_b_pw1[j],
                                conv_w_dw[j], conv_b_dw[j], conv_ln_g[j], conv_ln_b[j],
                                conv_w_pw2[j].astype(BF16), conv_b_pw2[j], d, plan)
            xt = _ffn_sublayer(xt, mod, norm_g[i, 1], ffn_w_gate[j].astype(BF16),
                               ffn_w_up[j].astype(BF16), ffn_w_down[j].astype(BF16), d, plan)
        else:
            wqkv = attn_w_qkv[j]
            nkv = n_kv * head_dim
            dup = lambda w: jnp.repeat(w.reshape(dm, n_kv, 1, head_dim), 2, axis=2).reshape(dm, 2 * nkv)
            w_qkvd = jnp.concatenate([wqkv[:, :nq], dup(wqkv[:, nq:nq + nkv]), dup(wqkv[:, nq + nkv:])],
                                     axis=1).astype(BF16)
            gain_row = jnp.concatenate([jnp.tile(attn_q_gain[j] * head_dim ** -0.5, n_heads),
                                        jnp.tile(attn_k_gain[j], 2 * n_kv)])[None]
            q, kd, vd = _qkv_project(xt, mod, norm_g[i, 0], w_qkvd, gain_row, head_block_ones, rope, d, plan)
            xt = _attention(q, kd, vd, attn_sinks[j], xt, mod, attn_w_o[j].astype(BF16), d, plan)
            xt = _moe_sublayer(xt, mod, norm_g[i, 1], moe_w_router[j], moe_b_router[j],
                               moe_wg, moe_wu, moe_wd, j, d, plan)
    return xt.reshape(bsz, seq, dm)
```

```python
import functools
from typing import NamedTuple

import jax
import jax.numpy as jnp
from jax import lax
from jax.experimental import pallas as pl
from jax.experimental.pallas import tpu as pltpu

F32 = jnp.float32
BF16 = jnp.bfloat16
I32 = jnp.int32

V7X_LANES = 128
V7X_SUBLANES = 8
V7X_MXU_DIM = 256
ROW_ALIGN = 2 * V7X_SUBLANES
V7X_VMEM_LIMIT_BYTES = 56 * 1024 * 1024

ROPE_THETA = 500000.0
EPS = 1e-6
ATTN_BLOCK = 128
TOP_K = 2
MASK_VALUE = -1e30


class Dims(NamedTuple):
    batch: int
    seq: int
    d_model: int
    depth: int
    conv_kernel: int
    n_heads: int
    n_kv_heads: int
    head_dim: int
    d_ff_dense: int
    n_experts: int
    d_ff_expert: int


class Plan(NamedTuple):
    tm: int
    tf_expert: int
    tg: int
    expert_sub: int
    tn_ada: int
    conv_rows: int
    conv_cols: int
    chunk: int


def _largest_tile(total, target, quantum):
    best = None
    t = quantum
    while t <= min(total, target):
        if total % t == 0:
            best = t
        t += quantum
    return best if best is not None else total


def _make_plan(d: Dims) -> Plan:
    tm = _largest_tile(d.seq, 512, ATTN_BLOCK)
    return Plan(
        tm=tm,
        tf_expert=_largest_tile(d.d_ff_expert, 1792, V7X_MXU_DIM),
        tg=_largest_tile(d.batch * d.seq, 1024, ATTN_BLOCK),
        expert_sub=2,
        tn_ada=_largest_tile(6 * d.d_model, 1536, V7X_LANES),
        conv_rows=64,
        conv_cols=V7X_LANES,
        chunk=ATTN_BLOCK,
    )


def _params(*semantics):
    return pltpu.CompilerParams(dimension_semantics=semantics,
                                vmem_limit_bytes=V7X_VMEM_LIMIT_BYTES)


def _dot(a, b):
    return jnp.dot(a, b, preferred_element_type=F32)


def _sigmoid(x):
    return 1.0 / (1.0 + jnp.exp(-x))


def _norm_mod(x, g_row, shift_row, scale_row):
    y = x * lax.rsqrt(jnp.mean(x * x, axis=-1, keepdims=True) + EPS)
    return (y * g_row) * (1.0 + scale_row) + shift_row


def _full(shape):
    return pl.BlockSpec(shape, lambda *_: (0,) * len(shape))


def _ada_body(c_ref, w_ref, b_ref, o_ref):
    c = c_ref[...]
    ca = c * _sigmoid(c)
    o_ref[0] = jnp.dot(ca, w_ref[0], preferred_element_type=F32,
                       precision=lax.Precision.HIGHEST) + b_ref[0]


def _ada_all_layers(c, w_ada, b_ada, plan):
    depth, dm, n6 = w_ada.shape
    bsz = c.shape[0]
    tn = plan.tn_ada
    return pl.pallas_call(
        _ada_body,
        grid=(depth, n6 // tn),
        in_specs=[_full((bsz, dm)),
                  pl.BlockSpec((1, dm, tn), lambda i, j: (i, 0, j)),
                  pl.BlockSpec((1, 1, tn), lambda i, j: (i, 0, j))],
        out_specs=pl.BlockSpec((1, bsz, tn), lambda i, j: (i, 0, j)),
        out_shape=jax.ShapeDtypeStruct((depth, bsz, n6), F32),
        compiler_params=_params("arbitrary", "arbitrary"),
        name="adaln_mod",
    )(c, w_ada, b_ada.reshape(depth, 1, n6))


def _rope_body(pos_ref, f_ref, m1_ref, m2_ref, c_ref, s1_ref, s2_ref):
    ang = pos_ref[...].astype(F32) * f_ref[...]
    s = jnp.sin(ang)
    c_ref[...] = jnp.cos(ang)
    s1_ref[...] = s * m1_ref[...]
    s2_ref[...] = s * m2_ref[...]


def _rope_tables(positions, d: Dims, plan):
    t_total = d.batch * d.seq
    rot = d.head_dim // 4
    half = rot // 2
    inv_freq = ROPE_THETA ** (-jnp.arange(0, rot, 2, dtype=F32) / rot)
    lane = jnp.arange(V7X_LANES)
    in_head = lane % d.head_dim
    freq_row = jnp.where(in_head < rot, inv_freq[in_head % half], 0.0).astype(F32)[None]
    m1 = jnp.where((in_head >= half) & (in_head < rot), 1.0, 0.0).astype(F32)[None]
    m2 = jnp.where(in_head < half, -1.0, 0.0).astype(F32)[None]
    tm = plan.tm
    row = pl.BlockSpec((tm, V7X_LANES), lambda i: (i, 0))
    tab = jax.ShapeDtypeStruct((t_total, V7X_LANES), F32)
    return pl.pallas_call(
        _rope_body,
        grid=(t_total // tm,),
        in_specs=[pl.BlockSpec((tm, 1), lambda i: (i, 0)),
                  _full((1, V7X_LANES)), _full((1, V7X_LANES)), _full((1, V7X_LANES))],
        out_specs=[row, row, row],
        out_shape=[tab, tab, tab],
        compiler_params=_params("arbitrary"),
        name="rope_tables",
    )(positions.reshape(t_total, 1), freq_row, m1, m2)


def _conv_body(x_ref, mod_ref, ng_ref, w1_ref, b1_ref, wd_ref, bd_ref, lg_ref, lb_ref,
               w2_ref, b2_ref, o_ref, ubuf, wpk, vbuf, *, dm, ktaps, halo, tiles_per_seq, rc, cb):
    m = pl.program_id(0)
    tm = x_ref.shape[0]
    x = x_ref[...]
    mod = mod_ref[0]
    h = _norm_mod(x, ng_ref[...], mod[0:1], mod[1:2]).astype(BF16)
    u = _dot(h, w1_ref[...]) + b1_ref[...]
    glu = u[:, :dm] * _sigmoid(u[:, dm:])
    half = dm // 2

    high16 = jnp.uint32(0xFFFF0000)

    def bf16_bits(a):
        return lax.bitcast_convert_type(a.astype(BF16).astype(F32), jnp.uint32)

    def pack2(a):
        return bf16_bits(a[:, half:]) | (bf16_bits(a[:, :half]) >> 16)

    @pl.when(m % tiles_per_seq == 0)
    def _():
        ubuf[0:halo, :] = jnp.zeros((halo, half), ubuf.dtype)

    ubuf[halo:halo + tm, :] = pack2(glu)
    wpk[...] = pack2(wd_ref[...])

    def chunk(ci, carry):
        r0 = pl.multiple_of(ci * rc, rc)
        lo, hi = [], []
        for cbi in range(half // cb):
            cols = slice(cbi * cb, (cbi + 1) * cb)
            win = pltpu.bitcast(ubuf[pl.ds(r0, rc + halo), cols], BF16)
            acc = None
            for r in range(V7X_SUBLANES):
                z = None
                for q in range(halo // V7X_SUBLANES):
                    dl = V7X_SUBLANES * q + r
                    if dl >= ktaps:
                        continue
                    s = halo - V7X_SUBLANES * (q + 1)
                    taps = jnp.broadcast_to(wpk[dl:dl + 1, cols], (rc + V7X_SUBLANES, cb))
                    term = pltpu.bitcast(taps, BF16) * win[2 * s:2 * (s + rc + V7X_SUBLANES), :]
                    z = term if z is None else z + term
                if z is None:
                    continue
                z = pltpu.bitcast(z, ubuf.dtype)
                if r:
                    z = pltpu.roll(z, r, 0)
                z = pltpu.bitcast(z[V7X_SUBLANES:V7X_SUBLANES + rc, :], BF16)
                acc = z if acc is None else acc + z
            acc = pltpu.bitcast(acc, ubuf.dtype)
            lo.append(lax.bitcast_convert_type(acc << 16, F32))
            hi.append(lax.bitcast_convert_type(acc & high16, F32))
        conv = jnp.concatenate(lo + hi, axis=1) + bd_ref[...]
        mu = jnp.mean(conv, axis=-1, keepdims=True)
        cen = conv - mu
        var = jnp.mean(cen * cen, axis=-1, keepdims=True)
        y = cen * lax.rsqrt(var + EPS) * lg_ref[...] + lb_ref[...]
        vbuf[pl.ds(r0, rc), :] = (y * _sigmoid(y)).astype(BF16)
        return carry

    lax.fori_loop(0, tm // rc, chunk, 0)
    ubuf[0:halo, :] = ubuf[tm:tm + halo, :]
    y2 = _dot(vbuf[...], w2_ref[...]) + b2_ref[...]
    o_ref[...] = x + mod[2:3] * y2


def _conv_sublayer(x, mod, ng, w1, b1, w_dw, b_dw, ln_g, ln_b, w2, b2, d: Dims, plan):
    t_total, dm = x.shape
    tm = plan.tm
    ktaps = w_dw.shape[0]
    halo = -(-ktaps // V7X_SUBLANES) * V7X_SUBLANES
    wd = jnp.zeros((halo, dm), F32).at[:ktaps].set(w_dw[::-1])
    tiles_per_seq = d.seq // tm
    row = pl.BlockSpec((tm, dm), lambda i: (i, 0))
    body = functools.partial(_conv_body, dm=dm, ktaps=ktaps, halo=halo,
                             tiles_per_seq=tiles_per_seq, rc=plan.conv_rows, cb=plan.conv_cols)
    return pl.pallas_call(
        body,
        grid=(t_total // tm,),
        in_specs=[row,
                  pl.BlockSpec((1, 6, dm), lambda i: (i // tiles_per_seq, 0, 0)),
                  _full((1, dm)), _full((dm, 2 * dm)), _full((1, 2 * dm)),
                  _full(wd.shape), _full((1, dm)), _full((1, dm)), _full((1, dm)),
                  _full((dm, dm)), _full((1, dm))],
        out_specs=row,
        out_shape=jax.ShapeDtypeStruct((t_total, dm), F32),
        scratch_shapes=[pltpu.VMEM((halo + tm, dm // 2), jnp.uint32),
                        pltpu.VMEM((halo, dm // 2), jnp.uint32), pltpu.VMEM((tm, dm), BF16)],
        compiler_params=_params("arbitrary"),
        name="conv_sublayer",
    )(x, mod, ng[None], w1, b1[None], wd, b_dw[None], ln_g[None], ln_b[None], w2, b2[None])


def _swiglu(h, wg, wu, wd):
    g = _dot(h, wg)
    u = _dot(h, wu)
    return _dot(((g * _sigmoid(g)) * u).astype(BF16), wd)


def _ffn_body(x_ref, mod_ref, ng_ref, wg_ref, wu_ref, wd_ref, o_ref):
    mod = mod_ref[0]
    x = x_ref[...]
    h = _norm_mod(x, ng_ref[...], mod[3:4], mod[4:5]).astype(BF16)
    o_ref[...] = x + mod[5:6] * _swiglu(h, wg_ref[...], wu_ref[...], wd_ref[...])


def _ffn_sublayer(x, mod, ng, wg, wu, wd, d: Dims, plan):
    t_total, dm = x.shape
    tm = plan.tm
    tiles_per_seq = d.seq // tm
    row = pl.BlockSpec((tm, dm), lambda i: (i, 0))
    resident = pl.BlockSpec(memory_space=pltpu.VMEM)
    return pl.pallas_call(
        _ffn_body,
        grid=(t_total // tm,),
        in_specs=[row,
                  pl.BlockSpec((1, 6, dm), lambda i: (i // tiles_per_seq, 0, 0)),
                  _full((1, dm)), resident, resident, resident],
        out_specs=row,
        out_shape=jax.ShapeDtypeStruct((t_total, dm), F32),
        compiler_params=_params("arbitrary"),
        name="dense_swiglu",
    )(x, mod, ng[None], wg, wu, wd)


def _expert_body(te_ref, nv_ref, xs_ref, wg_ref, wu_ref, wd_ref, ys_ref, acc_ref, *, n_sub):
    i = pl.program_id(0)
    f = pl.program_id(1)
    rows = xs_ref.shape[0] // n_sub

    @pl.when(i < nv_ref[0])
    def _():
        @pl.when(f == 0)
        def _():
            acc_ref[...] = jnp.zeros_like(acc_ref)

        for s in range(n_sub):
            sl = slice(s * rows, (s + 1) * rows)
            acc_ref[sl, :] += _swiglu(xs_ref[sl, :], wg_ref[0], wu_ref[0], wd_ref[0])

        @pl.when(f == pl.num_programs(1) - 1)
        def _():
            ys_ref[...] = acc_ref[...].astype(ys_ref.dtype)

    @pl.when((i >= nv_ref[0]) & (f == 0))
    def _():
        ys_ref[...] = jnp.zeros_like(ys_ref)


def _expert_swiglu(xs, tile_expert, n_valid, n_tiles, wg, wu, wd, plan):
    dm = xs.shape[1]
    ff = wg.shape[2]
    tg, tf = plan.tg, plan.tf_expert
    p_rows = n_tiles * tg
    nf = ff // tf

    def row_map(i, f, te, nv):
        return (jnp.minimum(i, nv[0] - 1), 0)

    def fsel(i, f, nv):
        return jnp.where(i < nv[0], f, nf - 1)

    grid_spec = pltpu.PrefetchScalarGridSpec(
        num_scalar_prefetch=2,
        grid=(p_rows // tg, nf),
        in_specs=[pl.BlockSpec((tg, dm), row_map),
                  pl.BlockSpec((1, dm, tf), lambda i, f, te, nv: (te[i], 0, fsel(i, f, nv))),
                  pl.BlockSpec((1, dm, tf), lambda i, f, te, nv: (te[i], 0, fsel(i, f, nv))),
                  pl.BlockSpec((1, tf, dm), lambda i, f, te, nv: (te[i], fsel(i, f, nv), 0))],
        out_specs=pl.BlockSpec((tg, dm), lambda i, f, te, nv: (i, 0)),
        scratch_shapes=[pltpu.VMEM((tg, dm), F32)],
    )
    return pl.pallas_call(
        functools.partial(_expert_body, n_sub=plan.expert_sub),
        grid_spec=grid_spec,
        out_shape=jax.ShapeDtypeStruct((p_rows, dm), BF16),
        compiler_params=_params("arbitrary", "arbitrary"),
        name="expert_swiglu",
    )(tile_expert, n_valid, xs, wg, wu, wd)


def _qkv_body(x_ref, mod_ref, ng_ref, w_ref, gain_ref, bd_ref, c_ref, s1_ref, s2_ref,
              q_ref, k_ref, v_ref, *, nq, nk, head_dim, half_rot):
    mod = mod_ref[0]
    h = _norm_mod(x_ref[...], ng_ref[...], mod[0:1], mod[1:2]).astype(BF16)
    qkv = _dot(h, w_ref[...])
    v_ref[...] = qkv[:, nq + nk:].astype(BF16)
    cos = c_ref[...]
    sin_hi = s1_ref[...]
    sin_lo = s2_ref[...]
    bd = bd_ref[...]
    for j in range((nq + nk) // V7X_MXU_DIM):
        blk = qkv[:, j * V7X_MXU_DIM:(j + 1) * V7X_MXU_DIM]
        ss = _dot((blk * blk).astype(BF16), bd)
        yn = (blk * lax.rsqrt(ss * (1.0 / head_dim) + EPS)) * gain_ref[:, j * V7X_MXU_DIM:(j + 1) * V7X_MXU_DIM]
        for p in range(V7X_MXU_DIM // V7X_LANES):
            xx = yn[:, p * V7X_LANES:(p + 1) * V7X_LANES]
            rot = (xx * cos + pltpu.roll(xx, half_rot, 1) * sin_hi
                   + pltpu.roll(xx, V7X_LANES - half_rot, 1) * sin_lo).astype(BF16)
            col = j * V7X_MXU_DIM + p * V7X_LANES
            if col < nq:
                q_ref[:, col:col + V7X_LANES] = rot
            else:
                k_ref[:, col - nq:col - nq + V7X_LANES] = rot


def _qkv_project(x, mod, ng, w_qkvd, gain_row, bd, rope, d: Dims, plan):
    t_total, dm = x.shape
    tm = plan.tm
    nq = d.n_heads * d.head_dim
    nk = d.n_kv_heads * 2 * d.head_dim
    tiles_per_seq = d.seq // tm
    row = pl.BlockSpec((tm, dm), lambda i: (i, 0))
    tab = pl.BlockSpec((tm, V7X_LANES), lambda i: (i, 0))
    body = functools.partial(_qkv_body, nq=nq, nk=nk, head_dim=d.head_dim,
                             half_rot=d.head_dim // 8)
    return pl.pallas_call(
        body,
        grid=(t_total // tm,),
        in_specs=[row,
                  pl.BlockSpec((1, 6, dm), lambda i: (i // tiles_per_seq, 0, 0)),
                  _full((1, dm)), _full((dm, nq + 2 * nk)), _full((1, nq + nk)),
                  _full((V7X_MXU_DIM, V7X_MXU_DIM)), tab, tab, tab],
        out_specs=[pl.BlockSpec((tm, nq), lambda i: (i, 0)),
                   pl.BlockSpec((tm, nk), lambda i: (i, 0)),
                   pl.BlockSpec((tm, nk), lambda i: (i, 0))],
        out_shape=[jax.ShapeDtypeStruct((t_total, nq), BF16),
                   jax.ShapeDtypeStruct((t_total, nk), BF16),
                   jax.ShapeDtypeStruct((t_total, nk), BF16)],
        compiler_params=_params("arbitrary"),
        name="qkv_project",
    )(x, mod, ng[None], w_qkvd, gain_row, bd, *rope)


def _attn_body(q_ref, k_ref, v_ref, kp_ref, vp_ref, sink_ref, x_ref, mod_ref, wo_ref, o_ref,
               o_scr, *, n_kv, group, head_dim, tiles_per_seq):
    m = pl.program_id(0)
    tq = q_ref.shape[0]
    blk = ATTN_BLOCK
    qi = lax.broadcasted_iota(I32, (blk, blk), 0)
    kj = lax.broadcasted_iota(I32, (blk, blk), 1)
    from_prev = kj > qi
    kmin = jnp.where(m % tiles_per_seq == 0, blk, 0)
    lo = lax.broadcasted_iota(I32, (blk, V7X_LANES), 1) < head_dim
    zero = jnp.zeros((blk, V7X_LANES), BF16)
    pairs = group // 2
    for n in range(tq // blk):
        rows = slice(n * blk, (n + 1) * blk)
        if n == 0:
            prev_ok = from_prev & (kj >= kmin)
            kprev, vprev = kp_ref[...], vp_ref[...]
        else:
            prev_ok = None
            prev = slice((n - 1) * blk, n * blk)
            kprev, vprev = k_ref[prev, :], v_ref[prev, :]
        kcur, vcur = k_ref[rows, :], v_ref[rows, :]
        for g in range(n_kv):
            gl = slice(g * V7X_LANES, (g + 1) * V7X_LANES)
            kk = jnp.concatenate([kprev[:, gl], kcur[:, gl]], axis=0)
            vv = jnp.concatenate([vprev[:, gl], vcur[:, gl]], axis=0)
            slabs = []
            for pr in range(pairs):
                c0 = (g * pairs + pr) * V7X_LANES
                qp = q_ref[rows, c0:c0 + V7X_LANES]
                slabs += [jnp.where(lo, qp, zero), jnp.where(lo, zero, qp)]
            qs = jnp.concatenate(slabs, axis=0)
            s = lax.dot_general(qs, kk, (((1,), (1,)), ((), ())), preferred_element_type=F32)
            ps, dens = [], []
            for hh in range(group):
                s_prev = s[hh * blk:(hh + 1) * blk, :blk]
                s_cur = s[hh * blk:(hh + 1) * blk, blk:]
                if prev_ok is None:
                    sh = jnp.where(from_prev, s_prev, s_cur)
                else:
                    sh = jnp.where(prev_ok, s_prev, jnp.where(from_prev, MASK_VALUE, s_cur))
                sink = sink_ref[0, g * group + hh]
                mx = jnp.maximum(jnp.max(sh, axis=-1, keepdims=True), sink)
                p = jnp.exp(sh - mx)
                dens.append(jnp.sum(p, axis=-1, keepdims=True) + jnp.exp(sink - mx))
                ps.append(jnp.concatenate([jnp.where(from_prev, p, 0.0), jnp.where(from_prev, 0.0, p)],
                                          axis=1).astype(BF16))
            o = _dot(jnp.concatenate(ps, axis=0), vv)
            for pr in range(pairs):
                h0, h1 = 2 * pr, 2 * pr + 1
                oa = o[h0 * blk:(h0 + 1) * blk, :] / dens[h0]
                ob = o[h1 * blk:(h1 + 1) * blk, :] / dens[h1]
                c0 = (g * pairs + pr) * V7X_LANES
                o_scr[rows, c0:c0 + V7X_LANES] = jnp.where(lo, oa, ob).astype(BF16)
    mod = mod_ref[0]
    o_ref[...] = x_ref[...] + mod[2:3] * _dot(o_scr[...], wo_ref[...])


def _attention(q, kd, vd, sinks, x, mod, wo, d: Dims, plan):
    t_total, dm = x.shape
    tq = plan.tm
    nq = q.shape[1]
    nk = kd.shape[1]
    tiles_per_seq = d.seq // tq
    bpt = tq // ATTN_BLOCK
    group = d.n_heads // d.n_kv_heads
    prev = pl.BlockSpec((ATTN_BLOCK, nk), lambda i: (jnp.maximum(i * bpt - 1, 0), 0))
    cur = pl.BlockSpec((tq, nk), lambda i: (i, 0))
    body = functools.partial(_attn_body, n_kv=d.n_kv_heads, group=group, head_dim=d.head_dim,
                             tiles_per_seq=tiles_per_seq)
    return pl.pallas_call(
        body,
        grid=(t_total // tq,),
        in_specs=[pl.BlockSpec((tq, nq), lambda i: (i, 0)), cur, cur, prev, prev,
                  pl.BlockSpec(memory_space=pltpu.SMEM),
                  pl.BlockSpec((tq, dm), lambda i: (i, 0)),
                  pl.BlockSpec((1, 6, dm), lambda i: (i // tiles_per_seq, 0, 0)),
                  _full((nq, dm))],
        out_specs=pl.BlockSpec((tq, dm), lambda i: (i, 0)),
        out_shape=jax.ShapeDtypeStruct((t_total, dm), F32),
        scratch_shapes=[pltpu.VMEM((tq, nq), BF16)],
        compiler_params=_params("arbitrary"),
        name="swa_attention",
    )(q, kd, vd, kd, vd, sinks[None], x, mod, wo)


def _router_body(x_ref, mod_ref, ng_ref, wr_ref, br_ref, ltri_ref, h_ref, meta_ref, cnt_ref,
                 *, n_experts):
    tt = x_ref.shape[0]
    mod = mod_ref[0]
    h2 = _norm_mod(x_ref[...], ng_ref[...], mod[3:4], mod[4:5])
    h_hi = h2.astype(BF16)
    h_ref[...] = h_hi
    h_lo = (h2 - h_hi.astype(F32)).astype(BF16)
    both = _dot(h_hi, wr_ref[...])
    logits = (both[:, :V7X_LANES] + both[:, V7X_LANES:]
              + _dot(h_lo, wr_ref[:, :V7X_LANES])) + br_ref[...]
    lane = lax.broadcasted_iota(I32, (tt, V7X_LANES), 1).astype(F32)
    neg = jnp.float32(-jnp.inf)
    logits = jnp.where(lane < n_experts, logits, neg)
    big = jnp.float32(V7X_LANES)
    m1 = jnp.max(logits, axis=-1, keepdims=True)
    i1 = jnp.min(jnp.where(logits == m1, lane, big), axis=-1, keepdims=True)
    rest = jnp.where(lane == i1, neg, logits)
    m2 = jnp.max(rest, axis=-1, keepdims=True)
    i2 = jnp.min(jnp.where(rest == m2, lane, big), axis=-1, keepdims=True)
    e2 = jnp.exp(m2 - m1)
    w1 = 1.0 / (1.0 + e2)
    w2 = e2 / (1.0 + e2)

    sel1 = lane == i1
    sel2 = lane == i2
    assigned = jnp.where(sel1 | sel2, 1.0, 0.0)
    before = _dot(ltri_ref[...], assigned.astype(BF16))
    r1 = jnp.sum(jnp.where(sel1, before, 0.0), axis=-1, keepdims=True)
    r2 = jnp.sum(jnp.where(sel2, before, 0.0), axis=-1, keepdims=True)
    cnt_ref[0] = jnp.broadcast_to(jnp.sum(assigned, axis=0, keepdims=True), cnt_ref.shape[1:])
    meta = jnp.zeros((tt, V7X_LANES), F32)
    for col, val in enumerate((i1, i2, r1, r2, w1, w2)):
        meta = jnp.where(lane == col, val, meta)
    meta_ref[...] = meta


META_EXPERT, META_RANK, META_GATE = 0, 2, 4


def _route(x, mod, ng, w_router, b_router, d: Dims, plan):
    t_total, dm = x.shape
    tt = plan.tm
    ne = d.n_experts
    n_rt = t_total // tt
    tiles_per_seq = d.seq // tt
    wr = jnp.zeros((dm, V7X_LANES), F32).at[:, :ne].set(w_router)
    wr_hi = wr.astype(BF16)
    wr = jnp.concatenate([wr_hi, (wr - wr_hi.astype(F32)).astype(BF16)], axis=1)
    br = jnp.zeros((1, V7X_LANES), F32).at[0, :ne].set(b_router)
    ltri = jnp.tril(jnp.ones((tt, tt), BF16), -1)
    row = pl.BlockSpec((tt, dm), lambda i: (i, 0))
    meta = pl.BlockSpec((tt, V7X_LANES), lambda i: (i, 0))
    return pl.pallas_call(
        functools.partial(_router_body, n_experts=ne),
        grid=(n_rt,),
        in_specs=[row,
                  pl.BlockSpec((1, 6, dm), lambda i: (i // tiles_per_seq, 0, 0)),
                  _full((1, dm)), _full((dm, 2 * V7X_LANES)), _full((1, V7X_LANES)), _full((tt, tt))],
        out_specs=[row, meta, pl.BlockSpec((1, V7X_SUBLANES, V7X_LANES), lambda i: (i, 0, 0))],
        out_shape=[jax.ShapeDtypeStruct((t_total, dm), BF16),
                   jax.ShapeDtypeStruct((t_total, V7X_LANES), F32),
                   jax.ShapeDtypeStruct((n_rt, V7X_SUBLANES, V7X_LANES), F32)],
        compiler_params=_params("arbitrary"),
        name="moe_router",
    )(x, mod, ng[None], wr, br, ltri)


def _chunks(count, ch):
    return lax.shift_right_logical(count + (ch - 1), jnp.int32(ch.bit_length() - 1))


def _stage_offsets(cnt_ref, base, n_experts, ch):
    offs = []
    total = jnp.int32(0)
    for e in range(n_experts):
        offs.append(total)
        total = total + _chunks(cnt_ref[base + e], ch) * ch
    return offs, total


def _stage_pos(meta, lanes, offs):
    off_row = jnp.zeros((1, lanes.shape[1]), F32)
    for e, off in enumerate(offs):
        off_row = jnp.where(lanes[0:1, :] == e, off.astype(F32), off_row)
    pos = []
    for k in range(TOP_K):
        expert = meta[:, META_EXPERT + k:META_EXPERT + k + 1]
        seg_off = jnp.sum(jnp.where(lanes == expert, off_row, 0.0), axis=-1, keepdims=True)
        pos.append(seg_off + meta[:, META_RANK + k:META_RANK + k + 1])
    return pos


def _dispatch_body(seg_ref, cnt_ref, zf_ref, nv_ref, h_ref, meta_ref, xs_hbm, stage, sem,
                   *, n_experts, ch, tg, n_fill_tiles):
    m = pl.program_id(0)
    tt = h_ref.shape[0]
    buf = m % 2

    def block_copy(b, src_row, dst_row, rows):
        return pltpu.make_async_copy(stage.at[b, pl.ds(src_row, rows), :],
                                     xs_hbm.at[pl.ds(dst_row, rows), :], sem.at[b])

    def drain(b, n_chunks):
        def one(i, c):
            block_copy(b, 0, 0, ch).wait()
            return c

        lax.fori_loop(0, n_chunks, one, 0)

    @pl.when(m == 0)
    def _():
        stage[0, 0:tg + ch, :] = jnp.zeros((tg + ch, stage.shape[2]), stage.dtype)
        for e in range(n_experts):
            cp = block_copy(0, 0, pl.multiple_of(zf_ref[e], ROW_ALIGN), tg + ch)
            cp.start()
            cp.wait()

        def fill(j, c):
            cp = block_copy(0, 0, pl.multiple_of(j * tg, tg), tg)
            cp.start()
            cp.wait()
            return c

        lax.fori_loop(nv_ref[0], n_fill_tiles, fill, 0)

    offs, total = _stage_offsets(cnt_ref, m * n_experts, n_experts, ch)
    lanes = lax.broadcasted_iota(I32, (tt, V7X_LANES), 1).astype(F32)
    pos = _stage_pos(meta_ref[...], lanes, offs)
    cols = jnp.full((tt, V7X_LANES), -1.0, F32)
    for k in range(TOP_K):
        cols = jnp.where(lanes == k, pos[k], cols)
    pos_rows = jnp.transpose(cols)
    slot = lax.broadcasted_iota(I32, (stage.shape[1], tt), 0).astype(F32)
    onehot = jnp.zeros(slot.shape, F32)
    for k in range(TOP_K):
        onehot = jnp.where(slot == pos_rows[k:k + 1, :], 1.0, onehot)
    stage[buf] = _dot(onehot.astype(BF16), h_ref[...]).astype(stage.dtype)

    @pl.when(m > 0)
    def _():
        _, prev_total = _stage_offsets(cnt_ref, jnp.maximum(m - 1, 0) * n_experts, n_experts, ch)
        drain(1 - buf, _chunks(prev_total, ch))

    for e in range(n_experts):
        start = seg_ref[m * n_experts + e]

        def send(c, carry, start=start, off=offs[e]):
            block_copy(buf, pl.multiple_of(off + c * ch, ch),
                       pl.multiple_of(start + c * ch, ROW_ALIGN), ch).start()
            return carry

        lax.fori_loop(0, _chunks(cnt_ref[m * n_experts + e], ch), send, 0)

    @pl.when(m == pl.num_programs(0) - 1)
    def _():
        drain(buf, _chunks(total, ch))


def _dispatch(h2, meta, seg_start, seg_cnt, zfill_start, n_valid, n_rows, d: Dims, plan):
    t_total, dm = h2.shape
    tt, tg, ch = plan.tm, plan.tg, plan.chunk
    ne = d.n_experts
    grid_spec = pltpu.PrefetchScalarGridSpec(
        num_scalar_prefetch=4,
        grid=(t_total // tt,),
        in_specs=[pl.BlockSpec((tt, dm), lambda i, *_: (i, 0)),
                  pl.BlockSpec((tt, V7X_LANES), lambda i, *_: (i, 0))],
        out_specs=pl.BlockSpec(memory_space=pl.ANY),
        scratch_shapes=[pltpu.VMEM((2, max(TOP_K * tt + ne * ch, tg + ch), dm), BF16),
                        pltpu.SemaphoreType.DMA((2,))],
    )
    body = functools.partial(_dispatch_body, n_experts=ne, ch=ch, tg=tg, n_fill_tiles=n_rows // tg)
    return pl.pallas_call(
        body,
        grid_spec=grid_spec,
        out_shape=jax.ShapeDtypeStruct((n_rows, dm), BF16),
        compiler_params=_params("arbitrary"),
        name="moe_dispatch",
    )(seg_start, seg_cnt, zfill_start, n_valid, h2, meta)


def _combine_body(seg_ref, cnt_ref, ys_hbm, x_ref, mod_ref, meta_ref, o_ref, stage, sem,
                  *, n_experts, ch):
    m = pl.program_id(0)
    tt = x_ref.shape[0]
    buf = m % 2

    def fetch(b, src_row, dst_row):
        return pltpu.make_async_copy(ys_hbm.at[pl.ds(src_row, ch), :],
                                     stage.at[b, pl.ds(dst_row, ch), :], sem.at[b])

    def fetch_tile(tile, b):
        offs, _ = _stage_offsets(cnt_ref, tile * n_experts, n_experts, ch)
        for e in range(n_experts):
            start = seg_ref[tile * n_experts + e]

            def issue(c, carry, start=start, off=offs[e]):
                fetch(b, pl.multiple_of(start + c * ch, ROW_ALIGN),
                      pl.multiple_of(off + c * ch, ch)).start()
                return carry

            lax.fori_loop(0, _chunks(cnt_ref[tile * n_experts + e], ch), issue, 0)

    @pl.when(m == 0)
    def _():
        stage[...] = jnp.zeros_like(stage)
        fetch_tile(m, buf)

    @pl.when(m + 1 < pl.num_programs(0))
    def _():
        fetch_tile(m + 1, 1 - buf)

    offs, total = _stage_offsets(cnt_ref, m * n_experts, n_experts, ch)
    meta = meta_ref[...]
    lanes = lax.broadcasted_iota(I32, (tt, V7X_LANES), 1).astype(F32)
    pos = _stage_pos(meta, lanes, offs)
    slot = lax.broadcasted_iota(I32, (tt, stage.shape[1]), 1).astype(F32)
    weights = jnp.zeros(slot.shape, F32)
    for k in range(TOP_K):
        weights = jnp.where(slot == pos[k], meta[:, META_GATE + k:META_GATE + k + 1], weights)

    def drain(i, c):
        fetch(buf, 0, 0).wait()
        return c

    lax.fori_loop(0, _chunks(total, ch), drain, 0)
    y = _dot(weights.astype(BF16), stage[buf])
    o_ref[...] = x_ref[...] + mod_ref[0][5:6] * y


def _combine(ys, seg_start, seg_cnt, x, mod, meta, d: Dims, plan):
    t_total, dm = x.shape
    tt, ch = plan.tm, plan.chunk
    ne = d.n_experts
    tiles_per_seq = d.seq // tt
    row = pl.BlockSpec((tt, dm), lambda i, *_: (i, 0))
    grid_spec = pltpu.PrefetchScalarGridSpec(
        num_scalar_prefetch=2,
        grid=(t_total // tt,),
        in_specs=[pl.BlockSpec(memory_space=pl.ANY), row,
                  pl.BlockSpec((1, 6, dm), lambda i, *_: (i // tiles_per_seq, 0, 0)),
                  pl.BlockSpec((tt, V7X_LANES), lambda i, *_: (i, 0))],
        out_specs=row,
        scratch_shapes=[pltpu.VMEM((2, TOP_K * tt + ne * ch, dm), BF16),
                        pltpu.SemaphoreType.DMA((2,))],
    )
    return pl.pallas_call(
        functools.partial(_combine_body, n_experts=ne, ch=ch),
        grid_spec=grid_spec,
        out_shape=jax.ShapeDtypeStruct((t_total, dm), F32),
        compiler_params=_params("arbitrary"),
        name="moe_combine",
    )(seg_start, seg_cnt, ys, x, mod, meta)


def _round_up(v, q):
    return (v + q - 1) // q * q


def _moe_sublayer(x, mod, ng, w_router, b_router, wg, wu, wd, layer, d: Dims, plan):
    t_total, dm = x.shape
    ne, tg, ch, tt = d.n_experts, plan.tg, plan.chunk, plan.tm
    n_rt = t_total // tt
    h2, meta, cnt = _route(x, mod, ng, w_router, b_router, d, plan)
    seg_cnt = cnt[:, 0, :ne].astype(I32)
    seg_len = _round_up(seg_cnt, ROW_ALIGN)
    used = jnp.sum(seg_len, axis=0)
    region = _round_up(used + ch, tg)
    region_end = jnp.cumsum(region)
    region_start = region_end - region
    seg_start = region_start[None, :] + jnp.cumsum(seg_len, axis=0) - seg_len
    n_valid = (region_end[ne - 1] // tg).astype(I32)
    n_tiles = -(-(TOP_K * t_total + n_rt * ne * (ROW_ALIGN - 1) + ne * ch) // tg) + ne
    tile_id = jnp.minimum(jnp.arange(n_tiles, dtype=I32), n_valid - 1)
    tile_expert = jnp.sum(tile_id[:, None] >= (region_end // tg)[None, :ne - 1], axis=1).astype(I32)
    tile_expert = tile_expert + layer * ne
    seg_start = seg_start.reshape(-1).astype(I32)
    seg_cnt = seg_cnt.reshape(-1)
    xs = _dispatch(h2, meta, seg_start, seg_cnt, (region_start + used).astype(I32), n_valid.reshape(1),
                   (n_tiles + 2) * tg, d, plan)
    ys = _expert_swiglu(xs, tile_expert, n_valid.reshape(1), n_tiles, wg, wu, wd, plan)
    return _combine(ys, seg_start, seg_cnt, x, mod, meta, d, plan)


def kernel(x, c, positions, norm_g, w_ada, b_ada, conv_w_pw1, conv_b_pw1, conv_w_dw, conv_b_dw, conv_ln_g, conv_ln_b, conv_w_pw2, conv_b_pw2, attn_w_qkv, attn_q_gain, attn_k_gain, attn_sinks, attn_w_o, ffn_w_gate, ffn_w_up, ffn_w_down, moe_w_router, moe_b_router, moe_w_gate, moe_w_up, moe_w_down):
    bsz, seq, dm = x.shape
    head_dim = attn_q_gain.shape[1]
    n_heads = attn_sinks.shape[1]
    n_kv = (attn_w_qkv.shape[2] // head_dim - n_heads) // 2
    d = Dims(batch=bsz, seq=seq, d_model=dm, depth=norm_g.shape[0], conv_kernel=conv_w_dw.shape[1],
             n_heads=n_heads, n_kv_heads=n_kv, head_dim=head_dim, d_ff_dense=ffn_w_gate.shape[2],
             n_experts=moe_w_router.shape[2], d_ff_expert=moe_w_gate.shape[3])
    assert 2 * head_dim == V7X_LANES and (n_heads // n_kv) % 2 == 0 and n_heads % n_kv == 0
    assert seq % ATTN_BLOCK == 0 and dm % V7X_MXU_DIM == 0
    assert (n_heads + 2 * n_kv) * head_dim % V7X_MXU_DIM == 0 and n_kv * 2 * head_dim % V7X_MXU_DIM == 0
    plan = _make_plan(d)
    t_total = bsz * seq
    nq = n_heads * head_dim

    mod_all = _ada_all_layers(c, w_ada, b_ada, plan).reshape(d.depth, bsz, 6, dm)
    rope = _rope_tables(positions, d, plan)
    blk_id = jnp.arange(V7X_MXU_DIM) // head_dim
    head_block_ones = (blk_id[:, None] == blk_id[None, :]).astype(BF16)
    moe_wg = moe_w_gate.astype(BF16).reshape((-1,) + moe_w_gate.shape[2:])
    moe_wu = moe_w_up.astype(BF16).reshape((-1,) + moe_w_up.shape[2:])
    moe_wd = moe_w_down.astype(BF16).reshape((-1,) + moe_w_down.shape[2:])

    xt = x.reshape(t_total, dm)
    for i in range(d.depth):
        j = i // 2
        mod = mod_all[i]
        if i % 2 == 0:
            xt = _conv_sublayer(xt, mod, norm_g[i, 0], conv_w_pw1[j].astype(BF16), conv_b_pw1[j],
                                conv_w_dw[j], conv_b_dw[j], conv_ln_g[j], conv_ln_b[j],
                                conv_w_pw2[j].astype(BF16), conv_b_pw2[j], d, plan)
            xt = _ffn_sublayer(xt, mod, norm_g[i, 1], ffn_w_gate[j].astype(BF16),
                               ffn_w_up[j].astype(BF16), ffn_w_down[j].astype(BF16), d, plan)
        else:
            wqkv = attn_w_qkv[j]
            nkv = n_kv * head_dim
            dup = lambda w: jnp.repeat(w.reshape(dm, n_kv, 1, head_dim), 2, axis=2).reshape(dm, 2 * nkv)
            w_qkvd = jnp.concatenate([wqkv[:, :nq], dup(wqkv[:, nq:nq + nkv]), dup(wqkv[:, nq + nkv:])],
                                     axis=1).astype(BF16)
            gain_row = jnp.concatenate([jnp.tile(attn_q_gain[j] * head_dim ** -0.5, n_heads),
                                        jnp.tile(attn_k_gain[j], 2 * n_kv)])[None]
            q, kd, vd = _qkv_project(xt, mod, norm_g[i, 0], w_qkvd, gain_row, head_block_ones, rope, d, plan)
            xt = _attention(q, kd, vd, attn_sinks[j], xt, mod, attn_w_o[j].astype(BF16), d, plan)
            xt = _moe_sublayer(xt, mod, norm_g[i, 1], moe_w_router[j], moe_b_router[j],
                               moe_wg, moe_wu, moe_wd, j, d, plan)
    return xt.reshape(bsz, seq, dm)
```

```python
import functools
from typing import NamedTuple

import jax
import jax.numpy as jnp
from jax import lax
from jax.experimental import pallas as pl
from jax.experimental.pallas import tpu as pltpu

F32 = jnp.float32
BF16 = jnp.bfloat16
I32 = jnp.int32

V7X_LANES = 128
V7X_SUBLANES = 8
V7X_MXU_DIM = 256
ROW_ALIGN = 2 * V7X_SUBLANES
V7X_VMEM_LIMIT_BYTES = 56 * 1024 * 1024

ROPE_THETA = 500000.0
EPS = 1e-6
ATTN_BLOCK = 128
TOP_K = 2
MASK_VALUE = -1e30


class Dims(NamedTuple):
    batch: int
    seq: int
    d_model: int
    depth: int
    conv_kernel: int
    n_heads: int
    n_kv_heads: int
    head_dim: int
    d_ff_dense: int
    n_experts: int
    d_ff_expert: int


class Plan(NamedTuple):
    tm: int
    tf_expert: int
    tg: int
    expert_sub: int
    tn_ada: int
    conv_rows: int
    conv_cols: int
    chunk: int


def _largest_tile(total, target, quantum):
    best = None
    t = quantum
    while t <= min(total, target):
        if total % t == 0:
            best = t
        t += quantum
    return best if best is not None else total


def _make_plan(d: Dims) -> Plan:
    tm = _largest_tile(d.seq, 512, ATTN_BLOCK)
    return Plan(
        tm=tm,
        tf_expert=_largest_tile(d.d_ff_expert, 1792, V7X_MXU_DIM),
        tg=_largest_tile(d.batch * d.seq, 1024, ATTN_BLOCK),
        expert_sub=2,
        tn_ada=_largest_tile(6 * d.d_model, 1536, V7X_LANES),
        conv_rows=64,
        conv_cols=V7X_LANES,
        chunk=ATTN_BLOCK // 2,
    )


def _params(*semantics):
    return pltpu.CompilerParams(dimension_semantics=semantics,
                                vmem_limit_bytes=V7X_VMEM_LIMIT_BYTES)


def _dot(a, b):
    return jnp.dot(a, b, preferred_element_type=F32)


def _sigmoid(x):
    return 1.0 / (1.0 + jnp.exp(-x))


def _norm_mod(x, g_row, shift_row, scale_row):
    y = x * lax.rsqrt(jnp.mean(x * x, axis=-1, keepdims=True) + EPS)
    return (y * g_row) * (1.0 + scale_row) + shift_row


def _full(shape):
    return pl.BlockSpec(shape, lambda *_: (0,) * len(shape))


def _ada_body(c_ref, w_ref, b_ref, o_ref):
    c = c_ref[...]
    ca = c * _sigmoid(c)
    o_ref[0] = jnp.dot(ca, w_ref[0], preferred_element_type=F32,
                       precision=lax.Precision.HIGHEST) + b_ref[0]


def _ada_all_layers(c, w_ada, b_ada, plan):
    depth, dm, n6 = w_ada.shape
    bsz = c.shape[0]
    tn = plan.tn_ada
    return pl.pallas_call(
        _ada_body,
        grid=(depth, n6 // tn),
        in_specs=[_full((bsz, dm)),
                  pl.BlockSpec((1, dm, tn), lambda i, j: (i, 0, j)),
                  pl.BlockSpec((1, 1, tn), lambda i, j: (i, 0, j))],
        out_specs=pl.BlockSpec((1, bsz, tn), lambda i, j: (i, 0, j)),
        out_shape=jax.ShapeDtypeStruct((depth, bsz, n6), F32),
        compiler_params=_params("arbitrary", "arbitrary"),
        name="adaln_mod",
    )(c, w_ada, b_ada.reshape(depth, 1, n6))


def _rope_body(pos_ref, f_ref, m1_ref, m2_ref, c_ref, s1_ref, s2_ref):
    ang = pos_ref[...].astype(F32) * f_ref[...]
    s = jnp.sin(ang)
    c_ref[...] = jnp.cos(ang)
    s1_ref[...] = s * m1_ref[...]
    s2_ref[...] = s * m2_ref[...]


def _rope_tables(positions, d: Dims, plan):
    t_total = d.batch * d.seq
    rot = d.head_dim // 4
    half = rot // 2
    inv_freq = ROPE_THETA ** (-jnp.arange(0, rot, 2, dtype=F32) / rot)
    lane = jnp.arange(V7X_LANES)
    in_head = lane % d.head_dim
    freq_row = jnp.where(in_head < rot, inv_freq[in_head % half], 0.0).astype(F32)[None]
    m1 = jnp.where((in_head >= half) & (in_head < rot), 1.0, 0.0).astype(F32)[None]
    m2 = jnp.where(in_head < half, -1.0, 0.0).astype(F32)[None]
    tm = plan.tm
    row = pl.BlockSpec((tm, V7X_LANES), lambda i: (i, 0))
    tab = jax.ShapeDtypeStruct((t_total, V7X_LANES), F32)
    return pl.pallas_call(
        _rope_body,
        grid=(t_total // tm,),
        in_specs=[pl.BlockSpec((tm, 1), lambda i: (i, 0)),
                  _full((1, V7X_LANES)), _full((1, V7X_LANES)), _full((1, V7X_LANES))],
        out_specs=[row, row, row],
        out_shape=[tab, tab, tab],
        compiler_params=_params("arbitrary"),
        name="rope_tables",
    )(positions.reshape(t_total, 1), freq_row, m1, m2)


def _cast_riders(refs, n_cast):
    for src, dst in zip(refs[:n_cast], refs[len(refs) - n_cast:]):
        dst[...] = src[...].astype(dst.dtype)


class Cast(NamedTuple):
    arrays: tuple
    layer: int
    n_layers: int


def _rider_specs(cast: Cast, n_steps):
    in_specs, out_specs, shapes = [], [], []
    for a in cast.arrays:
        layer_rows = a.shape[0] // cast.n_layers
        rows = layer_rows // n_steps
        assert rows * n_steps * cast.n_layers == a.shape[0] and rows % ROW_ALIGN == 0
        first = cast.layer * n_steps
        in_specs.append(pl.BlockSpec((rows, a.shape[1]), lambda i, first=first: (first + i, 0)))
        out_specs.append(pl.BlockSpec((rows, a.shape[1]), lambda i: (i, 0)))
        shapes.append(jax.ShapeDtypeStruct((layer_rows, a.shape[1]), BF16))
    return in_specs, out_specs, shapes


def _conv_body(x_ref, mod_ref, ng_ref, w1_ref, b1_ref, wd_ref, bd_ref, lg_ref, lb_ref,
               w2_ref, b2_ref, *rest, n_cast, dm, ktaps, halo, tiles_per_seq, rc, cb):
    ubuf, wpk, vbuf = rest[-3:]
    o_ref = rest[n_cast]
    _cast_riders(rest[:-3], n_cast)
    m = pl.program_id(0)
    tm = x_ref.shape[0]
    x = x_ref[...]
    mod = mod_ref[0]
    h = _norm_mod(x, ng_ref[...], mod[0:1], mod[1:2]).astype(BF16)
    u = _dot(h, w1_ref[...]) + b1_ref[...]
    glu = u[:, :dm] * _sigmoid(u[:, dm:])
    half = dm // 2

    high16 = jnp.uint32(0xFFFF0000)

    def bf16_bits(a):
        return lax.bitcast_convert_type(a.astype(BF16).astype(F32), jnp.uint32)

    def pack2(a):
        return bf16_bits(a[:, half:]) | (bf16_bits(a[:, :half]) >> 16)

    @pl.when(m % tiles_per_seq == 0)
    def _():
        ubuf[0:halo, :] = jnp.zeros((halo, half), ubuf.dtype)

    ubuf[halo:halo + tm, :] = pack2(glu)
    wpk[...] = pack2(wd_ref[...])

    def chunk(ci, carry):
        r0 = pl.multiple_of(ci * rc, rc)
        lo, hi = [], []
        for cbi in range(half // cb):
            cols = slice(cbi * cb, (cbi + 1) * cb)
            win = pltpu.bitcast(ubuf[pl.ds(r0, rc + halo), cols], BF16)
            acc = None
            for r in range(V7X_SUBLANES):
                z = None
                for q in range(halo // V7X_SUBLANES):
                    dl = V7X_SUBLANES * q + r
                    if dl >= ktaps:
                        continue
                    s = halo - V7X_SUBLANES * (q + 1)
                    taps = jnp.broadcast_to(wpk[dl:dl + 1, cols], (rc + V7X_SUBLANES, cb))
                    term = pltpu.bitcast(taps, BF16) * win[2 * s:2 * (s + rc + V7X_SUBLANES), :]
                    z = term if z is None else z + term
                if z is None:
                    continue
                z = pltpu.bitcast(z, ubuf.dtype)
                if r:
                    z = pltpu.roll(z, r, 0)
                z = pltpu.bitcast(z[V7X_SUBLANES:V7X_SUBLANES + rc, :], BF16)
                acc = z if acc is None else acc + z
            acc = pltpu.bitcast(acc, ubuf.dtype)
            lo.append(lax.bitcast_convert_type(acc << 16, F32))
            hi.append(lax.bitcast_convert_type(acc & high16, F32))
        conv = jnp.concatenate(lo + hi, axis=1) + bd_ref[...]
        mu = jnp.mean(conv, axis=-1, keepdims=True)
        cen = conv - mu
        var = jnp.mean(cen * cen, axis=-1, keepdims=True)
        y = cen * lax.rsqrt(var + EPS) * lg_ref[...] + lb_ref[...]
        vbuf[pl.ds(r0, rc), :] = (y * _sigmoid(y)).astype(BF16)
        return carry

    lax.fori_loop(0, tm // rc, chunk, 0)
    ubuf[0:halo, :] = ubuf[tm:tm + halo, :]
    y2 = _dot(vbuf[...], w2_ref[...]) + b2_ref[...]
    o_ref[...] = x + mod[2:3] * y2


def _conv_sublayer(x, mod, ng, w1, b1, w_dw, b_dw, ln_g, ln_b, w2, b2, cast, d: Dims, plan):
    t_total, dm = x.shape
    tm = plan.tm
    n_steps = t_total // tm
    ktaps = w_dw.shape[0]
    halo = -(-ktaps // V7X_SUBLANES) * V7X_SUBLANES
    wd = jnp.zeros((halo, dm), F32).at[:ktaps].set(w_dw[::-1])
    tiles_per_seq = d.seq // tm
    row = pl.BlockSpec((tm, dm), lambda i: (i, 0))
    cast_in, cast_out, cast_shapes = _rider_specs(cast, n_steps)
    body = functools.partial(_conv_body, n_cast=len(cast.arrays), dm=dm, ktaps=ktaps, halo=halo,
                             tiles_per_seq=tiles_per_seq, rc=plan.conv_rows, cb=plan.conv_cols)
    return pl.pallas_call(
        body,
        grid=(n_steps,),
        in_specs=[row,
                  pl.BlockSpec((1, 6, dm), lambda i: (i // tiles_per_seq, 0, 0)),
                  _full((1, dm)), _full((dm, 2 * dm)), _full((1, 2 * dm)),
                  _full(wd.shape), _full((1, dm)), _full((1, dm)), _full((1, dm)),
                  _full((dm, dm)), _full((1, dm))] + cast_in,
        out_specs=[row] + cast_out,
        out_shape=[jax.ShapeDtypeStruct((t_total, dm), F32)] + cast_shapes,
        scratch_shapes=[pltpu.VMEM((halo + tm, dm // 2), jnp.uint32),
                        pltpu.VMEM((halo, dm // 2), jnp.uint32), pltpu.VMEM((tm, dm), BF16)],
        compiler_params=_params("arbitrary"),
        name="conv_sublayer",
    )(x, mod, ng[None], w1, b1[None], wd, b_dw[None], ln_g[None], ln_b[None], w2, b2[None],
      *cast.arrays)


def _swiglu(h, wg, wu, wd):
    g = _dot(h, wg)
    u = _dot(h, wu)
    return _dot(((g * _sigmoid(g)) * u).astype(BF16), wd)


def _ffn_body(x_ref, mod_ref, ng_ref, wg_ref, wu_ref, wd_ref, *rest, n_cast):
    o_ref = rest[n_cast]
    _cast_riders(rest, n_cast)
    mod = mod_ref[0]
    x = x_ref[...]
    h = _norm_mod(x, ng_ref[...], mod[3:4], mod[4:5]).astype(BF16)
    o_ref[...] = x + mod[5:6] * _swiglu(h, wg_ref[...], wu_ref[...], wd_ref[...])


def _ffn_sublayer(x, mod, ng, wg, wu, wd, cast, d: Dims, plan):
    t_total, dm = x.shape
    tm = plan.tm
    n_steps = t_total // tm
    tiles_per_seq = d.seq // tm
    row = pl.BlockSpec((tm, dm), lambda i: (i, 0))
    resident = pl.BlockSpec(memory_space=pltpu.VMEM)
    cast_in, cast_out, cast_shapes = _rider_specs(cast, n_steps)
    return pl.pallas_call(
        functools.partial(_ffn_body, n_cast=len(cast.arrays)),
        grid=(n_steps,),
        in_specs=[row,
                  pl.BlockSpec((1, 6, dm), lambda i: (i // tiles_per_seq, 0, 0)),
                  _full((1, dm)), resident, resident, resident] + cast_in,
        out_specs=[row] + cast_out,
        out_shape=[jax.ShapeDtypeStruct((t_total, dm), F32)] + cast_shapes,
        compiler_params=_params("arbitrary"),
        name="dense_swiglu",
    )(x, mod, ng[None], wg, wu, wd, *cast.arrays)


def _expert_body(te_ref, nv_ref, xs_ref, wg_ref, wu_ref, wd_ref, ys_ref, acc_ref, *, n_sub):
    i = pl.program_id(0)
    f = pl.program_id(1)
    rows = xs_ref.shape[0] // n_sub

    @pl.when(i < nv_ref[0])
    def _():
        @pl.when(f == 0)
        def _():
            acc_ref[...] = jnp.zeros_like(acc_ref)

        for s in range(n_sub):
            sl = slice(s * rows, (s + 1) * rows)
            acc_ref[sl, :] += _swiglu(xs_ref[sl, :], wg_ref[0], wu_ref[0], wd_ref[0])

        @pl.when(f == pl.num_programs(1) - 1)
        def _():
            ys_ref[...] = acc_ref[...].astype(ys_ref.dtype)

    @pl.when((i >= nv_ref[0]) & (f == 0))
    def _():
        ys_ref[...] = jnp.zeros_like(ys_ref)


def _expert_swiglu(xs, tile_expert, n_valid, n_tiles, wg, wu, wd, plan):
    dm = xs.shape[1]
    ff = wg.shape[2]
    tg, tf = plan.tg, plan.tf_expert
    p_rows = n_tiles * tg
    nf = ff // tf

    def row_map(i, f, te, nv):
        return (jnp.minimum(i, nv[0] - 1), 0)

    def fsel(i, f, nv):
        return jnp.where(i < nv[0], f, nf - 1)

    grid_spec = pltpu.PrefetchScalarGridSpec(
        num_scalar_prefetch=2,
        grid=(p_rows // tg, nf),
        in_specs=[pl.BlockSpec((tg, dm), row_map),
                  pl.BlockSpec((1, dm, tf), lambda i, f, te, nv: (te[i], 0, fsel(i, f, nv))),
                  pl.BlockSpec((1, dm, tf), lambda i, f, te, nv: (te[i], 0, fsel(i, f, nv))),
                  pl.BlockSpec((1, tf, dm), lambda i, f, te, nv: (te[i], fsel(i, f, nv), 0))],
        out_specs=pl.BlockSpec((tg, dm), lambda i, f, te, nv: (i, 0)),
        scratch_shapes=[pltpu.VMEM((tg, dm), F32)],
    )
    return pl.pallas_call(
        functools.partial(_expert_body, n_sub=plan.expert_sub),
        grid_spec=grid_spec,
        out_shape=jax.ShapeDtypeStruct((p_rows, dm), BF16),
        compiler_params=_params("arbitrary", "arbitrary"),
        name="expert_swiglu",
    )(tile_expert, n_valid, xs, wg, wu, wd)


def _qkv_body(x_ref, mod_ref, ng_ref, w_ref, gain_ref, bd_ref, c_ref, s1_ref, s2_ref,
              q_ref, k_ref, v_ref, *, nq, nk, head_dim, half_rot):
    mod = mod_ref[0]
    h = _norm_mod(x_ref[...], ng_ref[...], mod[0:1], mod[1:2]).astype(BF16)
    qkv = _dot(h, w_ref[...])
    v_ref[...] = qkv[:, nq + nk:].astype(BF16)
    cos = c_ref[...]
    sin_hi = s1_ref[...]
    sin_lo = s2_ref[...]
    bd = bd_ref[...]
    for j in range((nq + nk) // V7X_MXU_DIM):
        blk = qkv[:, j * V7X_MXU_DIM:(j + 1) * V7X_MXU_DIM]
        ss = _dot((blk * blk).astype(BF16), bd)
        yn = (blk * lax.rsqrt(ss * (1.0 / head_dim) + EPS)) * gain_ref[:, j * V7X_MXU_DIM:(j + 1) * V7X_MXU_DIM]
        for p in range(V7X_MXU_DIM // V7X_LANES):
            xx = yn[:, p * V7X_LANES:(p + 1) * V7X_LANES]
            rot = (xx * cos + pltpu.roll(xx, half_rot, 1) * sin_hi
                   + pltpu.roll(xx, V7X_LANES - half_rot, 1) * sin_lo).astype(BF16)
            col = j * V7X_MXU_DIM + p * V7X_LANES
            if col < nq:
                q_ref[:, col:col + V7X_LANES] = rot
            else:
                k_ref[:, col - nq:col - nq + V7X_LANES] = rot


def _qkv_project(x, mod, ng, w_qkvd, gain_row, bd, rope, d: Dims, plan):
    t_total, dm = x.shape
    tm = plan.tm
    nq = d.n_heads * d.head_dim
    nk = d.n_kv_heads * 2 * d.head_dim
    tiles_per_seq = d.seq // tm
    row = pl.BlockSpec((tm, dm), lambda i: (i, 0))
    tab = pl.BlockSpec((tm, V7X_LANES), lambda i: (i, 0))
    body = functools.partial(_qkv_body, nq=nq, nk=nk, head_dim=d.head_dim,
                             half_rot=d.head_dim // 8)
    return pl.pallas_call(
        body,
        grid=(t_total // tm,),
        in_specs=[row,
                  pl.BlockSpec((1, 6, dm), lambda i: (i // tiles_per_seq, 0, 0)),
                  _full((1, dm)), _full((dm, nq + 2 * nk)), _full((1, nq + nk)),
                  _full((V7X_MXU_DIM, V7X_MXU_DIM)), tab, tab, tab],
        out_specs=[pl.BlockSpec((tm, nq), lambda i: (i, 0)),
                   pl.BlockSpec((tm, nk), lambda i: (i, 0)),
                   pl.BlockSpec((tm, nk), lambda i: (i, 0))],
        out_shape=[jax.ShapeDtypeStruct((t_total, nq), BF16),
                   jax.ShapeDtypeStruct((t_total, nk), BF16),
                   jax.ShapeDtypeStruct((t_total, nk), BF16)],
        compiler_params=_params("arbitrary"),
        name="qkv_project",
    )(x, mod, ng[None], w_qkvd, gain_row, bd, *rope)


def _attn_body(q_ref, k_ref, v_ref, kp_ref, vp_ref, sink_ref, x_ref, mod_ref, wo_ref, o_ref,
               o_scr, *, n_kv, group, head_dim, tiles_per_seq):
    m = pl.program_id(0)
    tq = q_ref.shape[0]
    blk = ATTN_BLOCK
    qi = lax.broadcasted_iota(I32, (blk, blk), 0)
    kj = lax.broadcasted_iota(I32, (blk, blk), 1)
    from_prev = kj > qi
    kmin = jnp.where(m % tiles_per_seq == 0, blk, 0)
    lo = lax.broadcasted_iota(I32, (blk, V7X_LANES), 1) < head_dim
    zero = jnp.zeros((blk, V7X_LANES), BF16)
    pairs = group // 2
    for n in range(tq // blk):
        rows = slice(n * blk, (n + 1) * blk)
        if n == 0:
            prev_ok = from_prev & (kj >= kmin)
            kprev, vprev = kp_ref[...], vp_ref[...]
        else:
            prev_ok = None
            prev = slice((n - 1) * blk, n * blk)
            kprev, vprev = k_ref[prev, :], v_ref[prev, :]
        kcur, vcur = k_ref[rows, :], v_ref[rows, :]
        for g in range(n_kv):
            gl = slice(g * V7X_LANES, (g + 1) * V7X_LANES)
            kk = jnp.concatenate([kprev[:, gl], kcur[:, gl]], axis=0)
            vv = jnp.concatenate([vprev[:, gl], vcur[:, gl]], axis=0)
            slabs = []
            for pr in range(pairs):
                c0 = (g * pairs + pr) * V7X_LANES
                qp = q_ref[rows, c0:c0 + V7X_LANES]
                slabs += [jnp.where(lo, qp, zero), jnp.where(lo, zero, qp)]
            qs = jnp.concatenate(slabs, axis=0)
            s = lax.dot_general(qs, kk, (((1,), (1,)), ((), ())), preferred_element_type=F32)
            ps, dens = [], []
            for hh in range(group):
                s_prev = s[hh * blk:(hh + 1) * blk, :blk]
                s_cur = s[hh * blk:(hh + 1) * blk, blk:]
                if prev_ok is None:
                    sh = jnp.where(from_prev, s_prev, s_cur)
                else:
                    sh = jnp.where(prev_ok, s_prev, jnp.where(from_prev, MASK_VALUE, s_cur))
                sink = sink_ref[0, g * group + hh]
                mx = jnp.maximum(jnp.max(sh, axis=-1, keepdims=True), sink)
                p = jnp.exp(sh - mx)
                dens.append(jnp.sum(p, axis=-1, keepdims=True) + jnp.exp(sink - mx))
                ps.append(jnp.concatenate([jnp.where(from_prev, p, 0.0), jnp.where(from_prev, 0.0, p)],
                                          axis=1).astype(BF16))
            o = _dot(jnp.concatenate(ps, axis=0), vv)
            for pr in range(pairs):
                h0, h1 = 2 * pr, 2 * pr + 1
                oa = o[h0 * blk:(h0 + 1) * blk, :] / dens[h0]
                ob = o[h1 * blk:(h1 + 1) * blk, :] / dens[h1]
                c0 = (g * pairs + pr) * V7X_LANES
                o_scr[rows, c0:c0 + V7X_LANES] = jnp.where(lo, oa, ob).astype(BF16)
    mod = mod_ref[0]
    o_ref[...] = x_ref[...] + mod[2:3] * _dot(o_scr[...], wo_ref[...])


def _attention(q, kd, vd, sinks, x, mod, wo, d: Dims, plan):
    t_total, dm = x.shape
    tq = plan.tm
    nq = q.shape[1]
    nk = kd.shape[1]
    tiles_per_seq = d.seq // tq
    bpt = tq // ATTN_BLOCK
    group = d.n_heads // d.n_kv_heads
    prev = pl.BlockSpec((ATTN_BLOCK, nk), lambda i: (jnp.maximum(i * bpt - 1, 0), 0))
    cur = pl.BlockSpec((tq, nk), lambda i: (i, 0))
    body = functools.partial(_attn_body, n_kv=d.n_kv_heads, group=group, head_dim=d.head_dim,
                             tiles_per_seq=tiles_per_seq)
    return pl.pallas_call(
        body,
        grid=(t_total // tq,),
        in_specs=[pl.BlockSpec((tq, nq), lambda i: (i, 0)), cur, cur, prev, prev,
                  pl.BlockSpec(memory_space=pltpu.SMEM),
                  pl.BlockSpec((tq, dm), lambda i: (i, 0)),
                  pl.BlockSpec((1, 6, dm), lambda i: (i // tiles_per_seq, 0, 0)),
                  _full((nq, dm))],
        out_specs=pl.BlockSpec((tq, dm), lambda i: (i, 0)),
        out_shape=jax.ShapeDtypeStruct((t_total, dm), F32),
        scratch_shapes=[pltpu.VMEM((tq, nq), BF16)],
        compiler_params=_params("arbitrary"),
        name="swa_attention",
    )(q, kd, vd, kd, vd, sinks[None], x, mod, wo)


def _router_body(x_ref, mod_ref, ng_ref, wr_ref, br_ref, ltri_ref, h_ref, meta_ref, cnt_ref,
                 *, n_experts):
    tt = x_ref.shape[0]
    mod = mod_ref[0]
    h2 = _norm_mod(x_ref[...], ng_ref[...], mod[3:4], mod[4:5])
    h_hi = h2.astype(BF16)
    h_ref[...] = h_hi
    h_lo = (h2 - h_hi.astype(F32)).astype(BF16)
    both = _dot(h_hi, wr_ref[...])
    logits = (both[:, :V7X_LANES] + both[:, V7X_LANES:]
              + _dot(h_lo, wr_ref[:, :V7X_LANES])) + br_ref[...]
    lane = lax.broadcasted_iota(I32, (tt, V7X_LANES), 1).astype(F32)
    neg = jnp.float32(-jnp.inf)
    logits = jnp.where(lane < n_experts, logits, neg)
    big = jnp.float32(V7X_LANES)
    m1 = jnp.max(logits, axis=-1, keepdims=True)
    i1 = jnp.min(jnp.where(logits == m1, lane, big), axis=-1, keepdims=True)
    rest = jnp.where(lane == i1, neg, logits)
    m2 = jnp.max(rest, axis=-1, keepdims=True)
    i2 = jnp.min(jnp.where(rest == m2, lane, big), axis=-1, keepdims=True)
    e2 = jnp.exp(m2 - m1)
    w1 = 1.0 / (1.0 + e2)
    w2 = e2 / (1.0 + e2)

    sel1 = lane == i1
    sel2 = lane == i2
    assigned = jnp.where(sel1 | sel2, 1.0, 0.0)
    before = _dot(ltri_ref[...], assigned.astype(BF16))
    r1 = jnp.sum(jnp.where(sel1, before, 0.0), axis=-1, keepdims=True)
    r2 = jnp.sum(jnp.where(sel2, before, 0.0), axis=-1, keepdims=True)
    cnt_ref[0] = jnp.broadcast_to(jnp.sum(assigned, axis=0, keepdims=True), cnt_ref.shape[1:])
    meta = jnp.zeros((tt, V7X_LANES), F32)
    for col, val in enumerate((i1, i2, r1, r2, w1, w2)):
        meta = jnp.where(lane == col, val, meta)
    meta_ref[...] = meta


META_EXPERT, META_RANK, META_GATE = 0, 2, 4


def _route(x, mod, ng, w_router, b_router, d: Dims, plan):
    t_total, dm = x.shape
    tt = plan.tm
    ne = d.n_experts
    n_rt = t_total // tt
    tiles_per_seq = d.seq // tt
    wr = jnp.zeros((dm, V7X_LANES), F32).at[:, :ne].set(w_router)
    wr_hi = wr.astype(BF16)
    wr = jnp.concatenate([wr_hi, (wr - wr_hi.astype(F32)).astype(BF16)], axis=1)
    br = jnp.zeros((1, V7X_LANES), F32).at[0, :ne].set(b_router)
    ltri = jnp.tril(jnp.ones((tt, tt), BF16), -1)
    row = pl.BlockSpec((tt, dm), lambda i: (i, 0))
    meta = pl.BlockSpec((tt, V7X_LANES), lambda i: (i, 0))
    return pl.pallas_call(
        functools.partial(_router_body, n_experts=ne),
        grid=(n_rt,),
        in_specs=[row,
                  pl.BlockSpec((1, 6, dm), lambda i: (i // tiles_per_seq, 0, 0)),
                  _full((1, dm)), _full((dm, 2 * V7X_LANES)), _full((1, V7X_LANES)), _full((tt, tt))],
        out_specs=[row, meta, pl.BlockSpec((1, V7X_SUBLANES, V7X_LANES), lambda i: (i, 0, 0))],
        out_shape=[jax.ShapeDtypeStruct((t_total, dm), BF16),
                   jax.ShapeDtypeStruct((t_total, V7X_LANES), F32),
                   jax.ShapeDtypeStruct((n_rt, V7X_SUBLANES, V7X_LANES), F32)],
        compiler_params=_params("arbitrary"),
        name="moe_router",
    )(x, mod, ng[None], wr, br, ltri)


def _chunks(count, ch):
    return lax.shift_right_logical(count + (ch - 1), jnp.int32(ch.bit_length() - 1))


def _stage_offsets(cnt_ref, base, n_experts, ch):
    offs = []
    total = jnp.int32(0)
    for e in range(n_experts):
        offs.append(total)
        total = total + _chunks(cnt_ref[base + e], ch) * ch
    return offs, total


def _stage_pos(meta, lanes, offs):
    off_row = jnp.zeros((1, lanes.shape[1]), F32)
    for e, off in enumerate(offs):
        off_row = jnp.where(lanes[0:1, :] == e, off.astype(F32), off_row)
    pos = []
    for k in range(TOP_K):
        expert = meta[:, META_EXPERT + k:META_EXPERT + k + 1]
        seg_off = jnp.sum(jnp.where(lanes == expert, off_row, 0.0), axis=-1, keepdims=True)
        pos.append(seg_off + meta[:, META_RANK + k:META_RANK + k + 1])
    return pos


def _dispatch_body(seg_ref, cnt_ref, zf_ref, nv_ref, h_ref, meta_ref, xs_hbm, stage, sem,
                   *, n_experts, ch, tg, n_fill_tiles):
    m = pl.program_id(0)
    tt = h_ref.shape[0]
    buf = m % 2

    def block_copy(b, src_row, dst_row, rows):
        return pltpu.make_async_copy(stage.at[b, pl.ds(src_row, rows), :],
                                     xs_hbm.at[pl.ds(dst_row, rows), :], sem.at[b])

    def drain(b, n_chunks):
        def one(i, c):
            block_copy(b, 0, 0, ch).wait()
            return c

        lax.fori_loop(0, n_chunks, one, 0)

    @pl.when(m == 0)
    def _():
        stage[0, 0:tg + ch, :] = jnp.zeros((tg + ch, stage.shape[2]), stage.dtype)
        for e in range(n_experts):
            cp = block_copy(0, 0, pl.multiple_of(zf_ref[e], ROW_ALIGN), tg + ch)
            cp.start()
            cp.wait()

        def fill(j, c):
            cp = block_copy(0, 0, pl.multiple_of(j * tg, tg), tg)
            cp.start()
            cp.wait()
            return c

        lax.fori_loop(nv_ref[0], n_fill_tiles, fill, 0)

    offs, total = _stage_offsets(cnt_ref, m * n_experts, n_experts, ch)
    lanes = lax.broadcasted_iota(I32, (tt, V7X_LANES), 1).astype(F32)
    pos = _stage_pos(meta_ref[...], lanes, offs)
    cols = jnp.full((tt, V7X_LANES), -1.0, F32)
    for k in range(TOP_K):
        cols = jnp.where(lanes == k, pos[k], cols)
    pos_rows = jnp.transpose(cols)
    slot = lax.broadcasted_iota(I32, (stage.shape[1], tt), 0).astype(F32)
    onehot = jnp.zeros(slot.shape, F32)
    for k in range(TOP_K):
        onehot = jnp.where(slot == pos_rows[k:k + 1, :], 1.0, onehot)
    stage[buf] = _dot(onehot.astype(BF16), h_ref[...]).astype(stage.dtype)

    @pl.when(m > 0)
    def _():
        _, prev_total = _stage_offsets(cnt_ref, jnp.maximum(m - 1, 0) * n_experts, n_experts, ch)
        drain(1 - buf, _chunks(prev_total, ch))

    for e in range(n_experts):
        start = seg_ref[m * n_experts + e]

        def send(c, carry, start=start, off=offs[e]):
            block_copy(buf, pl.multiple_of(off + c * ch, ch),
                       pl.multiple_of(start + c * ch, ROW_ALIGN), ch).start()
            return carry

        lax.fori_loop(0, _chunks(cnt_ref[m * n_experts + e], ch), send, 0)

    @pl.when(m == pl.num_programs(0) - 1)
    def _():
        drain(buf, _chunks(total, ch))


def _dispatch(h2, meta, seg_start, seg_cnt, zfill_start, n_valid, n_rows, d: Dims, plan):
    t_total, dm = h2.shape
    tt, tg, ch = plan.tm, plan.tg, plan.chunk
    ne = d.n_experts
    grid_spec = pltpu.PrefetchScalarGridSpec(
        num_scalar_prefetch=4,
        grid=(t_total // tt,),
        in_specs=[pl.BlockSpec((tt, dm), lambda i, *_: (i, 0)),
                  pl.BlockSpec((tt, V7X_LANES), lambda i, *_: (i, 0))],
        out_specs=pl.BlockSpec(memory_space=pl.ANY),
        scratch_shapes=[pltpu.VMEM((2, max(TOP_K * tt + ne * ch, tg + ch), dm), BF16),
                        pltpu.SemaphoreType.DMA((2,))],
    )
    body = functools.partial(_dispatch_body, n_experts=ne, ch=ch, tg=tg, n_fill_tiles=n_rows // tg)
    return pl.pallas_call(
        body,
        grid_spec=grid_spec,
        out_shape=jax.ShapeDtypeStruct((n_rows, dm), BF16),
        compiler_params=_params("arbitrary"),
        name="moe_dispatch",
    )(seg_start, seg_cnt, zfill_start, n_valid, h2, meta)


def _combine_body(seg_ref, cnt_ref, ys_hbm, x_ref, mod_ref, meta_ref, o_ref, stage, sem,
                  *, n_experts, ch):
    m = pl.program_id(0)
    tt = x_ref.shape[0]
    buf = m % 2

    def fetch(b, src_row, dst_row):
        return pltpu.make_async_copy(ys_hbm.at[pl.ds(src_row, ch), :],
                                     stage.at[b, pl.ds(dst_row, ch), :], sem.at[b])

    def fetch_tile(tile, b):
        offs, _ = _stage_offsets(cnt_ref, tile * n_experts, n_experts, ch)
        for e in range(n_experts):
            start = seg_ref[tile * n_experts + e]

            def issue(c, carry, start=start, off=offs[e]):
                fetch(b, pl.multiple_of(start + c * ch, ROW_ALIGN),
                      pl.multiple_of(off + c * ch, ch)).start()
                return carry

            lax.fori_loop(0, _chunks(cnt_ref[tile * n_experts + e], ch), issue, 0)

    @pl.when(m == 0)
    def _():
        stage[...] = jnp.zeros_like(stage)
        fetch_tile(m, buf)

    @pl.when(m + 1 < pl.num_programs(0))
    def _():
        fetch_tile(m + 1, 1 - buf)

    offs, total = _stage_offsets(cnt_ref, m * n_experts, n_experts, ch)
    meta = meta_ref[...]
    lanes = lax.broadcasted_iota(I32, (tt, V7X_LANES), 1).astype(F32)
    pos = _stage_pos(meta, lanes, offs)
    slot = lax.broadcasted_iota(I32, (tt, stage.shape[1]), 1).astype(F32)
    weights = jnp.zeros(slot.shape, F32)
    for k in range(TOP_K):
        weights = jnp.where(slot == pos[k], meta[:, META_GATE + k:META_GATE + k + 1], weights)

    def drain(i, c):
        fetch(buf, 0, 0).wait()
        return c

    lax.fori_loop(0, _chunks(total, ch), drain, 0)
    y = _dot(weights.astype(BF16), stage[buf])
    o_ref[...] = x_ref[...] + mod_ref[0][5:6] * y


def _combine(ys, seg_start, seg_cnt, x, mod, meta, d: Dims, plan):
    t_total, dm = x.shape
    tt, ch = plan.tm, plan.chunk
    ne = d.n_experts
    tiles_per_seq = d.seq // tt
    row = pl.BlockSpec((tt, dm), lambda i, *_: (i, 0))
    grid_spec = pltpu.PrefetchScalarGridSpec(
        num_scalar_prefetch=2,
        grid=(t_total // tt,),
        in_specs=[pl.BlockSpec(memory_space=pl.ANY), row,
                  pl.BlockSpec((1, 6, dm), lambda i, *_: (i // tiles_per_seq, 0, 0)),
                  pl.BlockSpec((tt, V7X_LANES), lambda i, *_: (i, 0))],
        out_specs=row,
        scratch_shapes=[pltpu.VMEM((2, TOP_K * tt + ne * ch, dm), BF16),
                        pltpu.SemaphoreType.DMA((2,))],
    )
    return pl.pallas_call(
        functools.partial(_combine_body, n_experts=ne, ch=ch),
        grid_spec=grid_spec,
        out_shape=jax.ShapeDtypeStruct((t_total, dm), F32),
        compiler_params=_params("arbitrary"),
        name="moe_combine",
    )(seg_start, seg_cnt, ys, x, mod, meta)


def _round_up(v, q):
    return (v + q - 1) // q * q


def _moe_sublayer(x, mod, ng, w_router, b_router, wg, wu, wd, d: Dims, plan):
    t_total, dm = x.shape
    ne, tg, ch, tt = d.n_experts, plan.tg, plan.chunk, plan.tm
    n_rt = t_total // tt
    h2, meta, cnt = _route(x, mod, ng, w_router, b_router, d, plan)
    seg_cnt = cnt[:, 0, :ne].astype(I32)
    seg_len = _round_up(seg_cnt, ROW_ALIGN)
    used = jnp.sum(seg_len, axis=0)
    region = _round_up(used + ch, tg)
    region_end = jnp.cumsum(region)
    region_start = region_end - region
    seg_start = region_start[None, :] + jnp.cumsum(seg_len, axis=0) - seg_len
    n_valid = (region_end[ne - 1] // tg).astype(I32)
    n_tiles = -(-(TOP_K * t_total + n_rt * ne * (ROW_ALIGN - 1) + ne * ch) // tg) + ne
    tile_id = jnp.minimum(jnp.arange(n_tiles, dtype=I32), n_valid - 1)
    tile_expert = jnp.sum(tile_id[:, None] >= (region_end // tg)[None, :ne - 1], axis=1).astype(I32)
    seg_start = seg_start.reshape(-1).astype(I32)
    seg_cnt = seg_cnt.reshape(-1)
    xs = _dispatch(h2, meta, seg_start, seg_cnt, (region_start + used).astype(I32), n_valid.reshape(1),
                   (n_tiles + 2) * tg, d, plan)
    ys = _expert_swiglu(xs, tile_expert, n_valid.reshape(1), n_tiles, wg, wu, wd, plan)
    return _combine(ys, seg_start, seg_cnt, x, mod, meta, d, plan)


def kernel(x, c, positions, norm_g, w_ada, b_ada, conv_w_pw1, conv_b_pw1, conv_w_dw, conv_b_dw, conv_ln_g, conv_ln_b, conv_w_pw2, conv_b_pw2, attn_w_qkv, attn_q_gain, attn_k_gain, attn_sinks, attn_w_o, ffn_w_gate, ffn_w_up, ffn_w_down, moe_w_router, moe_b_router, moe_w_gate, moe_w_up, moe_w_down):
    bsz, seq, dm = x.shape
    head_dim = attn_q_gain.shape[1]
    n_heads = attn_sinks.shape[1]
    n_kv = (attn_w_qkv.shape[2] // head_dim - n_heads) // 2
    d = Dims(batch=bsz, seq=seq, d_model=dm, depth=norm_g.shape[0], conv_kernel=conv_w_dw.shape[1],
             n_heads=n_heads, n_kv_heads=n_kv, head_dim=head_dim, d_ff_dense=ffn_w_gate.shape[2],
             n_experts=moe_w_router.shape[2], d_ff_expert=moe_w_gate.shape[3])
    assert 2 * head_dim == V7X_LANES and (n_heads // n_kv) % 2 == 0 and n_heads % n_kv == 0
    assert seq % ATTN_BLOCK == 0 and dm % V7X_MXU_DIM == 0
    assert (n_heads + 2 * n_kv) * head_dim % V7X_MXU_DIM == 0 and n_kv * 2 * head_dim % V7X_MXU_DIM == 0
    plan = _make_plan(d)
    t_total = bsz * seq
    nq = n_heads * head_dim

    mod_all = _ada_all_layers(c, w_ada, b_ada, plan).reshape(d.depth, bsz, 6, dm)
    rope = _rope_tables(positions, d, plan)
    blk_id = jnp.arange(V7X_MXU_DIM) // head_dim
    head_block_ones = (blk_id[:, None] == blk_id[None, :]).astype(BF16)
    n_moe = moe_w_gate.shape[0]
    ne, ffe = d.n_experts, d.d_ff_expert
    flat = lambda w: w.reshape(-1, w.shape[-1])
    moe_w16 = {}

    xt = x.reshape(t_total, dm)
    for i in range(d.depth):
        j = i // 2
        mod = mod_all[i]
        if i % 2 == 0:
            ride = j < n_moe
            gate_up = Cast((flat(moe_w_gate), flat(moe_w_up)) if ride else (), j, n_moe)
            down = Cast((flat(moe_w_down),) if ride else (), j, n_moe)
            xt, *gu16 = _conv_sublayer(xt, mod, norm_g[i, 0], conv_w_pw1[j].astype(BF16), conv_b_pw1[j],
                                       conv_w_dw[j], conv_b_dw[j], conv_ln_g[j], conv_ln_b[j],
                                       conv_w_pw2[j].astype(BF16), conv_b_pw2[j], gate_up, d, plan)
            xt, *dn16 = _ffn_sublayer(xt, mod, norm_g[i, 1], ffn_w_gate[j].astype(BF16),
                                      ffn_w_up[j].astype(BF16), ffn_w_down[j].astype(BF16), down, d, plan)
            if ride:
                moe_w16[j] = (gu16[0].reshape(ne, dm, ffe), gu16[1].reshape(ne, dm, ffe),
                              dn16[0].reshape(ne, ffe, dm))
        else:
            wqkv = attn_w_qkv[j]
            nkv = n_kv * head_dim
            dup = lambda w: jnp.repeat(w.reshape(dm, n_kv, 1, head_dim), 2, axis=2).reshape(dm, 2 * nkv)
            w_qkvd = jnp.concatenate([wqkv[:, :nq], dup(wqkv[:, nq:nq + nkv]), dup(wqkv[:, nq + nkv:])],
                                     axis=1).astype(BF16)
            gain_row = jnp.concatenate([jnp.tile(attn_q_gain[j] * head_dim ** -0.5, n_heads),
                                        jnp.tile(attn_k_gain[j], 2 * n_kv)])[None]
            q, kd, vd = _qkv_project(xt, mod, norm_g[i, 0], w_qkvd, gain_row, head_block_ones, rope, d, plan)
            xt = _attention(q, kd, vd, attn_sinks[j], xt, mod, attn_w_o[j].astype(BF16), d, plan)
            xt = _moe_sublayer(xt, mod, norm_g[i, 1], moe_w_router[j], moe_b_router[j],
                               *moe_w16[j], d, plan)
    return xt.reshape(bsz, seq, dm)
```

```python
import functools
from typing import NamedTuple

import jax
import jax.numpy as jnp
from jax import lax
from jax.experimental import pallas as pl
from jax.experimental.pallas import tpu as pltpu

F32 = jnp.float32
BF16 = jnp.bfloat16
I32 = jnp.int32

V7X_LANES = 128
V7X_SUBLANES = 8
V7X_MXU_DIM = 256
ROW_ALIGN = 2 * V7X_SUBLANES
V7X_VMEM_LIMIT_BYTES = 56 * 1024 * 1024

ROPE_THETA = 500000.0
EPS = 1e-6
ATTN_BLOCK = 128
TOP_K = 2
MASK_VALUE = -1e30


class Dims(NamedTuple):
    batch: int
    seq: int
    d_model: int
    depth: int
    conv_kernel: int
    n_heads: int
    n_kv_heads: int
    head_dim: int
    d_ff_dense: int
    n_experts: int
    d_ff_expert: int


class Plan(NamedTuple):
    tm: int
    tf_expert: int
    tg: int
    expert_sub: int
    proj_sub: int
    stage_sub: int
    tn_ada: int
    conv_rows: int
    conv_cols: int
    chunk: int


def _largest_tile(total, target, quantum):
    best = None
    t = quantum
    while t <= min(total, target):
        if total % t == 0:
            best = t
        t += quantum
    return best if best is not None else total


def _make_plan(d: Dims) -> Plan:
    tm = _largest_tile(d.seq, 512, ATTN_BLOCK)
    return Plan(
        tm=tm,
        tf_expert=_largest_tile(d.d_ff_expert, 1792, V7X_MXU_DIM),
        tg=_largest_tile(d.batch * d.seq, 1024, ATTN_BLOCK),
        expert_sub=2,
        proj_sub=4,
        stage_sub=1,
        tn_ada=_largest_tile(6 * d.d_model, 1536, V7X_LANES),
        conv_rows=64,
        conv_cols=V7X_LANES,
        chunk=ATTN_BLOCK // 2,
    )


def _params(*semantics):
    return pltpu.CompilerParams(dimension_semantics=semantics,
                                vmem_limit_bytes=V7X_VMEM_LIMIT_BYTES)


def _dot(a, b):
    return jnp.dot(a, b, preferred_element_type=F32)


def _sigmoid(x):
    return 1.0 / (1.0 + jnp.exp(-x))


def _norm_mod(x, g_row, shift_row, scale_row):
    y = x * lax.rsqrt(jnp.mean(x * x, axis=-1, keepdims=True) + EPS)
    return (y * g_row) * (1.0 + scale_row) + shift_row


def _full(shape):
    return pl.BlockSpec(shape, lambda *_: (0,) * len(shape))


def _ada_body(c_ref, w_ref, b_ref, o_ref):
    c = c_ref[...]
    ca = c * _sigmoid(c)
    o_ref[0] = jnp.dot(ca, w_ref[0], preferred_element_type=F32,
                       precision=lax.Precision.HIGHEST) + b_ref[0]


def _ada_all_layers(c, w_ada, b_ada, plan):
    depth, dm, n6 = w_ada.shape
    bsz = c.shape[0]
    tn = plan.tn_ada
    return pl.pallas_call(
        _ada_body,
        grid=(depth, n6 // tn),
        in_specs=[_full((bsz, dm)),
                  pl.BlockSpec((1, dm, tn), lambda i, j: (i, 0, j)),
                  pl.BlockSpec((1, 1, tn), lambda i, j: (i, 0, j))],
        out_specs=pl.BlockSpec((1, bsz, tn), lambda i, j: (i, 0, j)),
        out_shape=jax.ShapeDtypeStruct((depth, bsz, n6), F32),
        compiler_params=_params("arbitrary", "arbitrary"),
        name="adaln_mod",
    )(c, w_ada, b_ada.reshape(depth, 1, n6))


def _rope_body(pos_ref, f_ref, c_ref, s_ref):
    ang = pos_ref[...].astype(F32) * f_ref[...]
    c_ref[...] = jnp.cos(ang)
    s_ref[...] = jnp.sin(ang)


def _rope_tables(positions, d: Dims, plan):
    t_total = d.batch * d.seq
    rot = d.head_dim // 4
    half = rot // 2
    per_row = V7X_LANES // half
    n_rows = t_total // per_row
    assert n_rows * per_row == t_total and n_rows % V7X_SUBLANES == 0
    inv_freq = ROPE_THETA ** (-jnp.arange(0, rot, 2, dtype=F32) / rot)
    pos = jnp.repeat(positions.reshape(n_rows, per_row), half, axis=1)
    tab = jax.ShapeDtypeStruct((n_rows, V7X_LANES), F32)
    cos, sin = pl.pallas_call(
        _rope_body,
        in_specs=[_full((n_rows, V7X_LANES)), _full((1, V7X_LANES))],
        out_specs=[_full((n_rows, V7X_LANES)), _full((n_rows, V7X_LANES))],
        out_shape=[tab, tab],
        grid=(1,),
        compiler_params=_params("arbitrary"),
        name="rope_tables",
    )(pos, jnp.tile(inv_freq, per_row)[None])
    cos, sin = cos.reshape(t_total, half), sin.reshape(t_total, half)
    rest = d.head_dim - rot
    zeros = jnp.zeros((t_total, half), F32)
    pad0 = jnp.zeros((t_total, rest), F32)
    heads = V7X_LANES // d.head_dim
    c_tab = jnp.tile(jnp.concatenate([cos, cos, jnp.ones((t_total, rest), F32)], axis=1), (1, heads))
    s_hi = jnp.tile(jnp.concatenate([zeros, sin, pad0], axis=1), (1, heads))
    s_lo = jnp.tile(jnp.concatenate([-sin, zeros, pad0], axis=1), (1, heads))
    return c_tab, s_hi, s_lo


def _cast_riders(refs, n_cast):
    for src, dst in zip(refs[:n_cast], refs[len(refs) - n_cast:]):
        dst[...] = src[...].astype(dst.dtype)


class Cast(NamedTuple):
    arrays: tuple
    layer: int
    n_layers: int


def _rider_specs(cast: Cast, n_steps):
    in_specs, out_specs, shapes = [], [], []
    for a in cast.arrays:
        layer_rows = a.shape[0] // cast.n_layers
        rows = layer_rows // n_steps
        assert rows * n_steps * cast.n_layers == a.shape[0] and rows % ROW_ALIGN == 0
        first = cast.layer * n_steps
        in_specs.append(pl.BlockSpec((rows, a.shape[1]), lambda i, first=first: (first + i, 0)))
        out_specs.append(pl.BlockSpec((rows, a.shape[1]), lambda i: (i, 0)))
        shapes.append(jax.ShapeDtypeStruct((layer_rows, a.shape[1]), BF16))
    return in_specs, out_specs, shapes


def _conv_body(x_ref, mod_ref, ng_ref, w1_ref, b1_ref, wd_ref, bd_ref, lg_ref, lb_ref,
               w2_ref, b2_ref, *rest, n_cast, dm, ktaps, halo, tiles_per_seq, rc, cb, n_sub):
    ubuf, wpk, vbuf = rest[-3:]
    o_ref = rest[n_cast]
    _cast_riders(rest[:-3], n_cast)
    m = pl.program_id(0)
    tm = x_ref.shape[0]
    mod = mod_ref[0]
    half = dm // 2
    high16 = jnp.uint32(0xFFFF0000)

    def bf16_bits(a):
        return lax.bitcast_convert_type(a.astype(BF16).astype(F32), jnp.uint32)

    def pack2(a):
        return bf16_bits(a[:, half:]) | (bf16_bits(a[:, :half]) >> 16)

    @pl.when(m % tiles_per_seq == 0)
    def _():
        ubuf[0:halo, :] = jnp.zeros((halo, half), ubuf.dtype)

    sub = tm // n_sub
    for sb in range(n_sub):
        rows = slice(sb * sub, (sb + 1) * sub)
        h = _norm_mod(x_ref[rows, :], ng_ref[...], mod[0:1], mod[1:2]).astype(BF16)
        u = _dot(h, w1_ref[...]) + b1_ref[...]
        ubuf[halo + sb * sub:halo + (sb + 1) * sub, :] = pack2(u[:, :dm] * _sigmoid(u[:, dm:]))
    wpk[...] = pack2(wd_ref[...])

    def chunk(ci, carry):
        r0 = pl.multiple_of(ci * rc, rc)
        lo, hi = [], []
        for cbi in range(half // cb):
            cols = slice(cbi * cb, (cbi + 1) * cb)
            win = pltpu.bitcast(ubuf[pl.ds(r0, rc + halo), cols], BF16)
            acc = None
            for r in range(V7X_SUBLANES):
                z = None
                for q in range(halo // V7X_SUBLANES):
                    dl = V7X_SUBLANES * q + r
                    if dl >= ktaps:
                        continue
                    s = halo - V7X_SUBLANES * (q + 1)
                    taps = jnp.broadcast_to(wpk[dl:dl + 1, cols], (rc + V7X_SUBLANES, cb))
                    term = pltpu.bitcast(taps, BF16) * win[2 * s:2 * (s + rc + V7X_SUBLANES), :]
                    z = term if z is None else z + term
                if z is None:
                    continue
                z = pltpu.bitcast(z, ubuf.dtype)
                if r:
                    z = pltpu.roll(z, r, 0)
                z = pltpu.bitcast(z[V7X_SUBLANES:V7X_SUBLANES + rc, :], BF16)
                acc = z if acc is None else acc + z
            acc = pltpu.bitcast(acc, ubuf.dtype)
            lo.append(lax.bitcast_convert_type(acc << 16, F32))
            hi.append(lax.bitcast_convert_type(acc & high16, F32))
        conv = jnp.concatenate(lo + hi, axis=1) + bd_ref[...]
        mu = jnp.mean(conv, axis=-1, keepdims=True)
        cen = conv - mu
        var = jnp.mean(cen * cen, axis=-1, keepdims=True)
        y = cen * lax.rsqrt(var + EPS) * lg_ref[...] + lb_ref[...]
        vbuf[pl.ds(r0, rc), :] = (y * _sigmoid(y)).astype(BF16)
        return carry

    lax.fori_loop(0, tm // rc, chunk, 0)
    ubuf[0:halo, :] = ubuf[tm:tm + halo, :]
    y2 = _dot(vbuf[...], w2_ref[...]) + b2_ref[...]
    o_ref[...] = x_ref[...] + mod[2:3] * y2


def _conv_sublayer(x, mod, ng, w1, b1, w_dw, b_dw, ln_g, ln_b, w2, b2, cast, d: Dims, plan):
    t_total, dm = x.shape
    tm = plan.tm
    n_steps = t_total // tm
    ktaps = w_dw.shape[0]
    halo = -(-ktaps // V7X_SUBLANES) * V7X_SUBLANES
    wd = jnp.zeros((halo, dm), F32).at[:ktaps].set(w_dw[::-1])
    tiles_per_seq = d.seq // tm
    row = pl.BlockSpec((tm, dm), lambda i: (i, 0))
    cast_in, cast_out, cast_shapes = _rider_specs(cast, n_steps)
    body = functools.partial(_conv_body, n_cast=len(cast.arrays), dm=dm, ktaps=ktaps, halo=halo,
                             tiles_per_seq=tiles_per_seq, rc=plan.conv_rows, cb=plan.conv_cols,
                             n_sub=1)
    return pl.pallas_call(
        body,
        grid=(n_steps,),
        in_specs=[row,
                  pl.BlockSpec((1, 6, dm), lambda i: (i // tiles_per_seq, 0, 0)),
                  _full((1, dm)), _full((dm, 2 * dm)), _full((1, 2 * dm)),
                  _full(wd.shape), _full((1, dm)), _full((1, dm)), _full((1, dm)),
                  _full((dm, dm)), _full((1, dm))] + cast_in,
        out_specs=[row] + cast_out,
        out_shape=[jax.ShapeDtypeStruct((t_total, dm), F32)] + cast_shapes,
        scratch_shapes=[pltpu.VMEM((halo + tm, dm // 2), jnp.uint32),
                        pltpu.VMEM((halo, dm // 2), jnp.uint32), pltpu.VMEM((tm, dm), BF16)],
        compiler_params=_params("arbitrary"),
        name="conv_sublayer",
    )(x, mod, ng[None], w1, b1[None], wd, b_dw[None], ln_g[None], ln_b[None], w2, b2[None],
      *cast.arrays)


def _swiglu(h, wg, wu, wd):
    g = _dot(h, wg)
    u = _dot(h, wu)
    return _dot(((g * _sigmoid(g)) * u).astype(BF16), wd)


def _ffn_body(x_ref, mod_ref, ng_ref, wg_ref, wu_ref, wd_ref, *rest, n_cast):
    o_ref = rest[n_cast]
    _cast_riders(rest, n_cast)
    mod = mod_ref[0]
    x = x_ref[...]
    h = _norm_mod(x, ng_ref[...], mod[3:4], mod[4:5]).astype(BF16)
    o_ref[...] = x + mod[5:6] * _swiglu(h, wg_ref[...], wu_ref[...], wd_ref[...])


def _ffn_sublayer(x, mod, ng, wg, wu, wd, cast, d: Dims, plan):
    t_total, dm = x.shape
    tm = plan.tm
    n_steps = t_total // tm
    tiles_per_seq = d.seq // tm
    row = pl.BlockSpec((tm, dm), lambda i: (i, 0))
    resident = pl.BlockSpec(memory_space=pltpu.VMEM)
    cast_in, cast_out, cast_shapes = _rider_specs(cast, n_steps)
    return pl.pallas_call(
        functools.partial(_ffn_body, n_cast=len(cast.arrays)),
        grid=(n_steps,),
        in_specs=[row,
                  pl.BlockSpec((1, 6, dm), lambda i: (i // tiles_per_seq, 0, 0)),
                  _full((1, dm)), resident, resident, resident] + cast_in,
        out_specs=[row] + cast_out,
        out_shape=[jax.ShapeDtypeStruct((t_total, dm), F32)] + cast_shapes,
        compiler_params=_params("arbitrary"),
        name="dense_swiglu",
    )(x, mod, ng[None], wg, wu, wd, *cast.arrays)


def _expert_body(te_ref, nv_ref, xs_ref, wg_ref, wu_ref, wd_ref, ys_ref, acc_ref, *, n_sub):
    i = pl.program_id(0)
    f = pl.program_id(1)
    rows = xs_ref.shape[0] // n_sub

    @pl.when(i < nv_ref[0])
    def _():
        @pl.when(f == 0)
        def _():
            acc_ref[...] = jnp.zeros_like(acc_ref)

        for s in range(n_sub):
            sl = slice(s * rows, (s + 1) * rows)
            acc_ref[sl, :] += _swiglu(xs_ref[sl, :], wg_ref[0], wu_ref[0], wd_ref[0])

        @pl.when(f == pl.num_programs(1) - 1)
        def _():
            ys_ref[...] = acc_ref[...].astype(ys_ref.dtype)

    @pl.when((i >= nv_ref[0]) & (f == 0))
    def _():
        ys_ref[...] = jnp.zeros_like(ys_ref)


def _expert_swiglu(xs, tile_expert, n_valid, n_tiles, wg, wu, wd, plan):
    dm = xs.shape[1]
    ff = wg.shape[2]
    tg, tf = plan.tg, plan.tf_expert
    p_rows = n_tiles * tg
    nf = ff // tf

    def row_map(i, f, te, nv):
        return (jnp.minimum(i, nv[0] - 1), 0)

    def fsel(i, f, nv):
        return jnp.where(i < nv[0], f, nf - 1)

    grid_spec = pltpu.PrefetchScalarGridSpec(
        num_scalar_prefetch=2,
        grid=(p_rows // tg, nf),
        in_specs=[pl.BlockSpec((tg, dm), row_map),
                  pl.BlockSpec((1, dm, tf), lambda i, f, te, nv: (te[i], 0, fsel(i, f, nv))),
                  pl.BlockSpec((1, dm, tf), lambda i, f, te, nv: (te[i], 0, fsel(i, f, nv))),
                  pl.BlockSpec((1, tf, dm), lambda i, f, te, nv: (te[i], fsel(i, f, nv), 0))],
        out_specs=pl.BlockSpec((tg, dm), lambda i, f, te, nv: (i, 0)),
        scratch_shapes=[pltpu.VMEM((tg, dm), F32)],
    )
    return pl.pallas_call(
        functools.partial(_expert_body, n_sub=plan.expert_sub),
        grid_spec=grid_spec,
        out_shape=jax.ShapeDtypeStruct((p_rows, dm), BF16),
        compiler_params=_params("arbitrary", "arbitrary"),
        name="expert_swiglu",
    )(tile_expert, n_valid, xs, wg, wu, wd)


def _qkv_body(x_ref, mod_ref, ng_ref, w_ref, gain_ref, bd_ref, c_ref, s1_ref, s2_ref,
              q_ref, k_ref, v_ref, *, nq, nk, head_dim, half_rot, n_sub):
    mod = mod_ref[0]
    bd = bd_ref[...]
    sub = x_ref.shape[0] // n_sub
    for sb in range(n_sub):
        rows = slice(sb * sub, (sb + 1) * sub)
        h = _norm_mod(x_ref[rows, :], ng_ref[...], mod[0:1], mod[1:2]).astype(BF16)
        qkv = _dot(h, w_ref[...])
        v_ref[rows, :] = qkv[:, nq + nk:].astype(BF16)
        cos = c_ref[rows, :]
        sin_hi = s1_ref[rows, :]
        sin_lo = s2_ref[rows, :]
        for j in range((nq + nk) // V7X_MXU_DIM):
            blk = qkv[:, j * V7X_MXU_DIM:(j + 1) * V7X_MXU_DIM]
            ss = _dot((blk * blk).astype(BF16), bd)
            yn = ((blk * lax.rsqrt(ss * (1.0 / head_dim) + EPS))
                  * gain_ref[:, j * V7X_MXU_DIM:(j + 1) * V7X_MXU_DIM])
            for p in range(V7X_MXU_DIM // V7X_LANES):
                xx = yn[:, p * V7X_LANES:(p + 1) * V7X_LANES]
                rot = (xx * cos + pltpu.roll(xx, half_rot, 1) * sin_hi
                       + pltpu.roll(xx, V7X_LANES - half_rot, 1) * sin_lo).astype(BF16)
                col = j * V7X_MXU_DIM + p * V7X_LANES
                if col < nq:
                    q_ref[rows, col:col + V7X_LANES] = rot
                else:
                    k_ref[rows, col - nq:col - nq + V7X_LANES] = rot


def _qkv_project(x, mod, ng, w_qkvd, gain_row, bd, rope, d: Dims, plan):
    t_total, dm = x.shape
    tm = plan.tm
    nq = d.n_heads * d.head_dim
    nk = d.n_kv_heads * 2 * d.head_dim
    tiles_per_seq = d.seq // tm
    row = pl.BlockSpec((tm, dm), lambda i: (i, 0))
    tab = pl.BlockSpec((tm, V7X_LANES), lambda i: (i, 0))
    body = functools.partial(_qkv_body, nq=nq, nk=nk, head_dim=d.head_dim,
                             half_rot=d.head_dim // 8, n_sub=plan.proj_sub)
    return pl.pallas_call(
        body,
        grid=(t_total // tm,),
        in_specs=[row,
                  pl.BlockSpec((1, 6, dm), lambda i: (i // tiles_per_seq, 0, 0)),
                  _full((1, dm)), _full((dm, nq + 2 * nk)), _full((1, nq + nk)),
                  _full((V7X_MXU_DIM, V7X_MXU_DIM)), tab, tab, tab],
        out_specs=[pl.BlockSpec((tm, nq), lambda i: (i, 0)),
                   pl.BlockSpec((tm, nk), lambda i: (i, 0)),
                   pl.BlockSpec((tm, nk), lambda i: (i, 0))],
        out_shape=[jax.ShapeDtypeStruct((t_total, nq), BF16),
                   jax.ShapeDtypeStruct((t_total, nk), BF16),
                   jax.ShapeDtypeStruct((t_total, nk), BF16)],
        compiler_params=_params("arbitrary"),
        name="qkv_project",
    )(x, mod, ng[None], w_qkvd, gain_row, bd, *rope)


def _attn_body(q_ref, k_ref, v_ref, kp_ref, vp_ref, sink_ref, x_ref, mod_ref, wo_ref, o_ref,
               o_scr, *, n_kv, group, head_dim, tiles_per_seq):
    m = pl.program_id(0)
    tq = q_ref.shape[0]
    blk = ATTN_BLOCK
    qi = lax.broadcasted_iota(I32, (blk, blk), 0)
    kj = lax.broadcasted_iota(I32, (blk, blk), 1)
    from_prev = kj > qi
    kmin = jnp.where(m % tiles_per_seq == 0, blk, 0)
    lo = lax.broadcasted_iota(I32, (blk, V7X_LANES), 1) < head_dim
    zero = jnp.zeros((blk, V7X_LANES), BF16)
    pairs = group // 2
    for n in range(tq // blk):
        rows = slice(n * blk, (n + 1) * blk)
        if n == 0:
            prev_ok = from_prev & (kj >= kmin)
            kprev, vprev = kp_ref[...], vp_ref[...]
        else:
            prev_ok = None
            prev = slice((n - 1) * blk, n * blk)
            kprev, vprev = k_ref[prev, :], v_ref[prev, :]
        kcur, vcur = k_ref[rows, :], v_ref[rows, :]
        for g in range(n_kv):
            gl = slice(g * V7X_LANES, (g + 1) * V7X_LANES)
            kk = jnp.concatenate([kprev[:, gl], kcur[:, gl]], axis=0)
            vv = jnp.concatenate([vprev[:, gl], vcur[:, gl]], axis=0)
            slabs = []
            for pr in range(pairs):
                c0 = (g * pairs + pr) * V7X_LANES
                qp = q_ref[rows, c0:c0 + V7X_LANES]
                slabs += [jnp.where(lo, qp, zero), jnp.where(lo, zero, qp)]
            qs = jnp.concatenate(slabs, axis=0)
            s = lax.dot_general(qs, kk, (((1,), (1,)), ((), ())), preferred_element_type=F32)
            ps, dens = [], []
            for hh in range(group):
                s_prev = s[hh * blk:(hh + 1) * blk, :blk]
                s_cur = s[hh * blk:(hh + 1) * blk, blk:]
                if prev_ok is None:
                    sh = jnp.where(from_prev, s_prev, s_cur)
                else:
                    sh = jnp.where(prev_ok, s_prev, jnp.where(from_prev, MASK_VALUE, s_cur))
                sink = sink_ref[0, g * group + hh]
                mx = jnp.maximum(jnp.max(sh, axis=-1, keepdims=True), sink)
                p = jnp.exp(sh - mx)
                dens.append(jnp.sum(p, axis=-1, keepdims=True) + jnp.exp(sink - mx))
                ps.append(jnp.concatenate([jnp.where(from_prev, p, 0.0), jnp.where(from_prev, 0.0, p)],
                                          axis=1).astype(BF16))
            o = _dot(jnp.concatenate(ps, axis=0), vv)
            for pr in range(pairs):
                h0, h1 = 2 * pr, 2 * pr + 1
                oa = o[h0 * blk:(h0 + 1) * blk, :] / dens[h0]
                ob = o[h1 * blk:(h1 + 1) * blk, :] / dens[h1]
                c0 = (g * pairs + pr) * V7X_LANES
                o_scr[rows, c0:c0 + V7X_LANES] = jnp.where(lo, oa, ob).astype(BF16)
    mod = mod_ref[0]
    o_ref[...] = x_ref[...] + mod[2:3] * _dot(o_scr[...], wo_ref[...])


def _attention(q, kd, vd, sinks, x, mod, wo, d: Dims, plan):
    t_total, dm = x.shape
    tq = plan.tm
    nq = q.shape[1]
    nk = kd.shape[1]
    tiles_per_seq = d.seq // tq
    bpt = tq // ATTN_BLOCK
    group = d.n_heads // d.n_kv_heads
    prev = pl.BlockSpec((ATTN_BLOCK, nk), lambda i: (jnp.maximum(i * bpt - 1, 0), 0))
    cur = pl.BlockSpec((tq, nk), lambda i: (i, 0))
    body = functools.partial(_attn_body, n_kv=d.n_kv_heads, group=group, head_dim=d.head_dim,
                             tiles_per_seq=tiles_per_seq)
    return pl.pallas_call(
        body,
        grid=(t_total // tq,),
        in_specs=[pl.BlockSpec((tq, nq), lambda i: (i, 0)), cur, cur, prev, prev,
                  pl.BlockSpec(memory_space=pltpu.SMEM),
                  pl.BlockSpec((tq, dm), lambda i: (i, 0)),
                  pl.BlockSpec((1, 6, dm), lambda i: (i // tiles_per_seq, 0, 0)),
                  _full((nq, dm))],
        out_specs=pl.BlockSpec((tq, dm), lambda i: (i, 0)),
        out_shape=jax.ShapeDtypeStruct((t_total, dm), F32),
        scratch_shapes=[pltpu.VMEM((tq, nq), BF16)],
        compiler_params=_params("arbitrary"),
        name="swa_attention",
    )(q, kd, vd, kd, vd, sinks[None], x, mod, wo)


def _router_body(x_ref, mod_ref, ng_ref, wr_ref, br_ref, ltri_ref, h_ref, meta_ref, cnt_ref,
                 *, n_experts):
    tt = x_ref.shape[0]
    mod = mod_ref[0]
    h2 = _norm_mod(x_ref[...], ng_ref[...], mod[3:4], mod[4:5])
    h_hi = h2.astype(BF16)
    h_ref[...] = h_hi
    h_lo = (h2 - h_hi.astype(F32)).astype(BF16)
    both = _dot(h_hi, wr_ref[...])
    logits = (both[:, :V7X_LANES] + both[:, V7X_LANES:]
              + _dot(h_lo, wr_ref[:, :V7X_LANES])) + br_ref[...]
    lane = lax.broadcasted_iota(I32, (tt, V7X_LANES), 1).astype(F32)
    neg = jnp.float32(-jnp.inf)
    logits = jnp.where(lane < n_experts, logits, neg)
    big = jnp.float32(V7X_LANES)
    m1 = jnp.max(logits, axis=-1, keepdims=True)
    i1 = jnp.min(jnp.where(logits == m1, lane, big), axis=-1, keepdims=True)
    rest = jnp.where(lane == i1, neg, logits)
    m2 = jnp.max(rest, axis=-1, keepdims=True)
    i2 = jnp.min(jnp.where(rest == m2, lane, big), axis=-1, keepdims=True)
    e2 = jnp.exp(m2 - m1)
    w1 = 1.0 / (1.0 + e2)
    w2 = e2 / (1.0 + e2)

    sel1 = lane == i1
    sel2 = lane == i2
    assigned = jnp.where(sel1 | sel2, 1.0, 0.0)
    before = _dot(ltri_ref[...], assigned.astype(BF16))
    r1 = jnp.sum(jnp.where(sel1, before, 0.0), axis=-1, keepdims=True)
    r2 = jnp.sum(jnp.where(sel2, before, 0.0), axis=-1, keepdims=True)
    cnt_ref[0] = jnp.broadcast_to(jnp.sum(assigned, axis=0, keepdims=True), cnt_ref.shape[1:])
    meta = jnp.zeros((tt, V7X_LANES), F32)
    for col, val in enumerate((i1, i2, r1, r2, w1, w2)):
        meta = jnp.where(lane == col, val, meta)
    meta_ref[...] = meta


META_EXPERT, META_RANK, META_GATE = 0, 2, 4


def _route(x, mod, ng, w_router, b_router, d: Dims, plan):
    t_total, dm = x.shape
    tt = plan.tm
    ne = d.n_experts
    n_rt = t_total // tt
    tiles_per_seq = d.seq // tt
    wr = jnp.zeros((dm, V7X_LANES), F32).at[:, :ne].set(w_router)
    wr_hi = wr.astype(BF16)
    wr = jnp.concatenate([wr_hi, (wr - wr_hi.astype(F32)).astype(BF16)], axis=1)
    br = jnp.zeros((1, V7X_LANES), F32).at[0, :ne].set(b_router)
    ltri = jnp.tril(jnp.ones((tt, tt), BF16), -1)
    row = pl.BlockSpec((tt, dm), lambda i: (i, 0))
    meta = pl.BlockSpec((tt, V7X_LANES), lambda i: (i, 0))
    return pl.pallas_call(
        functools.partial(_router_body, n_experts=ne),
        grid=(n_rt,),
        in_specs=[row,
                  pl.BlockSpec((1, 6, dm), lambda i: (i // tiles_per_seq, 0, 0)),
                  _full((1, dm)), _full((dm, 2 * V7X_LANES)), _full((1, V7X_LANES)), _full((tt, tt))],
        out_specs=[row, meta, pl.BlockSpec((1, V7X_SUBLANES, V7X_LANES), lambda i: (i, 0, 0))],
        out_shape=[jax.ShapeDtypeStruct((t_total, dm), BF16),
                   jax.ShapeDtypeStruct((t_total, V7X_LANES), F32),
                   jax.ShapeDtypeStruct((n_rt, V7X_SUBLANES, V7X_LANES), F32)],
        compiler_params=_params("arbitrary"),
        name="moe_router",
    )(x, mod, ng[None], wr, br, ltri)


def _chunks(count, ch):
    return lax.shift_right_logical(count + (ch - 1), jnp.int32(ch.bit_length() - 1))


def _stage_offsets(cnt_ref, base, n_experts, ch):
    offs = []
    total = jnp.int32(0)
    for e in range(n_experts):
        offs.append(total)
        total = total + _chunks(cnt_ref[base + e], ch) * ch
    return offs, total


def _stage_pos(meta, lanes, offs):
    off_row = jnp.zeros((1, lanes.shape[1]), F32)
    for e, off in enumerate(offs):
        off_row = jnp.where(lanes[0:1, :] == e, off.astype(F32), off_row)
    pos = []
    for k in range(TOP_K):
        expert = meta[:, META_EXPERT + k:META_EXPERT + k + 1]
        seg_off = jnp.sum(jnp.where(lanes == expert, off_row, 0.0), axis=-1, keepdims=True)
        pos.append(seg_off + meta[:, META_RANK + k:META_RANK + k + 1])
    return pos


def _dispatch_body(seg_ref, cnt_ref, zf_ref, nv_ref, h_ref, meta_ref, xs_hbm, stage, sem,
                   *, n_experts, ch, tg, n_fill_tiles, n_sub):
    m = pl.program_id(0)
    tt = h_ref.shape[0]
    buf = m % 2

    def block_copy(b, src_row, dst_row, rows):
        return pltpu.make_async_copy(stage.at[b, pl.ds(src_row, rows), :],
                                     xs_hbm.at[pl.ds(dst_row, rows), :], sem.at[b])

    def drain(b, n_chunks):
        def one(i, c):
            block_copy(b, 0, 0, ch).wait()
            return c

        lax.fori_loop(0, n_chunks, one, 0)

    @pl.when(m == 0)
    def _():
        stage[0, 0:tg + ch, :] = jnp.zeros((tg + ch, stage.shape[2]), stage.dtype)
        for e in range(n_experts):
            cp = block_copy(0, 0, pl.multiple_of(zf_ref[e], ROW_ALIGN), tg + ch)
            cp.start()
            cp.wait()

        def fill(j, c):
            cp = block_copy(0, 0, pl.multiple_of(j * tg, tg), tg)
            cp.start()
            cp.wait()
            return c

        lax.fori_loop(nv_ref[0], n_fill_tiles, fill, 0)

    offs, total = _stage_offsets(cnt_ref, m * n_experts, n_experts, ch)
    lanes = lax.broadcasted_iota(I32, (tt, V7X_LANES), 1).astype(F32)
    pos = _stage_pos(meta_ref[...], lanes, offs)
    cols = jnp.full((tt, V7X_LANES), -1.0, F32)
    for k in range(TOP_K):
        cols = jnp.where(lanes == k, pos[k], cols)
    pos_rows = jnp.transpose(cols)
    sub = stage.shape[1] // n_sub
    for sb in range(n_sub):
        slot = (lax.broadcasted_iota(I32, (sub, tt), 0) + sb * sub).astype(F32)
        onehot = jnp.zeros(slot.shape, F32)
        for k in range(TOP_K):
            onehot = jnp.where(slot == pos_rows[k:k + 1, :], 1.0, onehot)
        stage[buf, sb * sub:(sb + 1) * sub, :] = _dot(onehot.astype(BF16), h_ref[...]).astype(stage.dtype)

    @pl.when(m > 0)
    def _():
        _, prev_total = _stage_offsets(cnt_ref, jnp.maximum(m - 1, 0) * n_experts, n_experts, ch)
        drain(1 - buf, _chunks(prev_total, ch))

    for e in range(n_experts):
        start = seg_ref[m * n_experts + e]

        def send(c, carry, start=start, off=offs[e]):
            block_copy(buf, pl.multiple_of(off + c * ch, ch),
                       pl.multiple_of(start + c * ch, ROW_ALIGN), ch).start()
            return carry

        lax.fori_loop(0, _chunks(cnt_ref[m * n_experts + e], ch), send, 0)

    @pl.when(m == pl.num_programs(0) - 1)
    def _():
        drain(buf, _chunks(total, ch))


def _dispatch(h2, meta, seg_start, seg_cnt, zfill_start, n_valid, n_rows, d: Dims, plan):
    t_total, dm = h2.shape
    tt, tg, ch = plan.tm, plan.tg, plan.chunk
    ne = d.n_experts
    grid_spec = pltpu.PrefetchScalarGridSpec(
        num_scalar_prefetch=4,
        grid=(t_total // tt,),
        in_specs=[pl.BlockSpec((tt, dm), lambda i, *_: (i, 0)),
                  pl.BlockSpec((tt, V7X_LANES), lambda i, *_: (i, 0))],
        out_specs=pl.BlockSpec(memory_space=pl.ANY),
        scratch_shapes=[pltpu.VMEM((2, max(TOP_K * tt + ne * ch, tg + ch), dm), BF16),
                        pltpu.SemaphoreType.DMA((2,))],
    )
    body = functools.partial(_dispatch_body, n_experts=ne, ch=ch, tg=tg, n_fill_tiles=n_rows // tg,
                             n_sub=plan.stage_sub)
    return pl.pallas_call(
        body,
        grid_spec=grid_spec,
        out_shape=jax.ShapeDtypeStruct((n_rows, dm), BF16),
        compiler_params=_params("arbitrary"),
        name="moe_dispatch",
    )(seg_start, seg_cnt, zfill_start, n_valid, h2, meta)


def _combine_body(seg_ref, cnt_ref, ys_hbm, x_ref, mod_ref, meta_ref, o_ref, stage, sem,
                  *, n_experts, ch, n_sub):
    m = pl.program_id(0)
    tt = x_ref.shape[0]
    buf = m % 2

    def fetch(b, src_row, dst_row):
        return pltpu.make_async_copy(ys_hbm.at[pl.ds(src_row, ch), :],
                                     stage.at[b, pl.ds(dst_row, ch), :], sem.at[b])

    def fetch_tile(tile, b):
        offs, _ = _stage_offsets(cnt_ref, tile * n_experts, n_experts, ch)
        for e in range(n_experts):
            start = seg_ref[tile * n_experts + e]

            def issue(c, carry, start=start, off=offs[e]):
                fetch(b, pl.multiple_of(start + c * ch, ROW_ALIGN),
                      pl.multiple_of(off + c * ch, ch)).start()
                return carry

            lax.fori_loop(0, _chunks(cnt_ref[tile * n_experts + e], ch), issue, 0)

    @pl.when(m == 0)
    def _():
        stage[...] = jnp.zeros_like(stage)
        fetch_tile(m, buf)

    @pl.when(m + 1 < pl.num_programs(0))
    def _():
        fetch_tile(m + 1, 1 - buf)

    offs, total = _stage_offsets(cnt_ref, m * n_experts, n_experts, ch)
    meta = meta_ref[...]
    lanes = lax.broadcasted_iota(I32, (tt, V7X_LANES), 1).astype(F32)
    pos = _stage_pos(meta, lanes, offs)
    def drain(i, c):
        fetch(buf, 0, 0).wait()
        return c

    lax.fori_loop(0, _chunks(total, ch), drain, 0)
    sub = tt // n_sub
    slot = lax.broadcasted_iota(I32, (sub, stage.shape[1]), 1).astype(F32)
    for sb in range(n_sub):
        rows = slice(sb * sub, (sb + 1) * sub)
        weights = jnp.zeros(slot.shape, F32)
        for k in range(TOP_K):
            weights = jnp.where(slot == pos[k][rows, :], meta[rows, META_GATE + k:META_GATE + k + 1], weights)
        y = _dot(weights.astype(BF16), stage[buf])
        o_ref[rows, :] = x_ref[rows, :] + mod_ref[0][5:6] * y


def _combine(ys, seg_start, seg_cnt, x, mod, meta, d: Dims, plan):
    t_total, dm = x.shape
    tt, ch = plan.tm, plan.chunk
    ne = d.n_experts
    tiles_per_seq = d.seq // tt
    row = pl.BlockSpec((tt, dm), lambda i, *_: (i, 0))
    grid_spec = pltpu.PrefetchScalarGridSpec(
        num_scalar_prefetch=2,
        grid=(t_total // tt,),
        in_specs=[pl.BlockSpec(memory_space=pl.ANY), row,
                  pl.BlockSpec((1, 6, dm), lambda i, *_: (i // tiles_per_seq, 0, 0)),
                  pl.BlockSpec((tt, V7X_LANES), lambda i, *_: (i, 0))],
        out_specs=row,
        scratch_shapes=[pltpu.VMEM((2, TOP_K * tt + ne * ch, dm), BF16),
                        pltpu.SemaphoreType.DMA((2,))],
    )
    return pl.pallas_call(
        functools.partial(_combine_body, n_experts=ne, ch=ch, n_sub=plan.proj_sub),
        grid_spec=grid_spec,
        out_shape=jax.ShapeDtypeStruct((t_total, dm), F32),
        compiler_params=_params("arbitrary"),
        name="moe_combine",
    )(seg_start, seg_cnt, ys, x, mod, meta)


def _round_up(v, q):
    return (v + q - 1) // q * q


def _moe_sublayer(x, mod, ng, w_router, b_router, wg, wu, wd, d: Dims, plan):
    t_total, dm = x.shape
    ne, tg, ch, tt = d.n_experts, plan.tg, plan.chunk, plan.tm
    n_rt = t_total // tt
    h2, meta, cnt = _route(x, mod, ng, w_router, b_router, d, plan)
    seg_cnt = cnt[:, 0, :ne].astype(I32)
    seg_len = _round_up(seg_cnt, ROW_ALIGN)
    used = jnp.sum(seg_len, axis=0)
    region = _round_up(used + ch, tg)
    region_end = jnp.cumsum(region)
    region_start = region_end - region
    seg_start = region_start[None, :] + jnp.cumsum(seg_len, axis=0) - seg_len
    n_valid = (region_end[ne - 1] // tg).astype(I32)
    n_tiles = -(-(TOP_K * t_total + n_rt * ne * (ROW_ALIGN - 1) + ne * ch) // tg) + ne
    tile_id = jnp.minimum(jnp.arange(n_tiles, dtype=I32), n_valid - 1)
    tile_expert = jnp.sum(tile_id[:, None] >= (region_end // tg)[None, :ne - 1], axis=1).astype(I32)
    seg_start = seg_start.reshape(-1).astype(I32)
    seg_cnt = seg_cnt.reshape(-1)
    xs = _dispatch(h2, meta, seg_start, seg_cnt, (region_start + used).astype(I32), n_valid.reshape(1),
                   (n_tiles + 2) * tg, d, plan)
    ys = _expert_swiglu(xs, tile_expert, n_valid.reshape(1), n_tiles, wg, wu, wd, plan)
    return _combine(ys, seg_start, seg_cnt, x, mod, meta, d, plan)


def kernel(x, c, positions, norm_g, w_ada, b_ada, conv_w_pw1, conv_b_pw1, conv_w_dw, conv_b_dw, conv_ln_g, conv_ln_b, conv_w_pw2, conv_b_pw2, attn_w_qkv, attn_q_gain, attn_k_gain, attn_sinks, attn_w_o, ffn_w_gate, ffn_w_up, ffn_w_down, moe_w_router, moe_b_router, moe_w_gate, moe_w_up, moe_w_down):
    bsz, seq, dm = x.shape
    head_dim = attn_q_gain.shape[1]
    n_heads = attn_sinks.shape[1]
    n_kv = (attn_w_qkv.shape[2] // head_dim - n_heads) // 2
    d = Dims(batch=bsz, seq=seq, d_model=dm, depth=norm_g.shape[0], conv_kernel=conv_w_dw.shape[1],
             n_heads=n_heads, n_kv_heads=n_kv, head_dim=head_dim, d_ff_dense=ffn_w_gate.shape[2],
             n_experts=moe_w_router.shape[2], d_ff_expert=moe_w_gate.shape[3])
    assert 2 * head_dim == V7X_LANES and (n_heads // n_kv) % 2 == 0 and n_heads % n_kv == 0
    assert seq % ATTN_BLOCK == 0 and dm % V7X_MXU_DIM == 0
    assert (n_heads + 2 * n_kv) * head_dim % V7X_MXU_DIM == 0 and n_kv * 2 * head_dim % V7X_MXU_DIM == 0
    plan = _make_plan(d)
    t_total = bsz * seq
    nq = n_heads * head_dim

    mod_all = _ada_all_layers(c, w_ada, b_ada, plan).reshape(d.depth, bsz, 6, dm)
    rope = _rope_tables(positions, d, plan)
    blk_id = jnp.arange(V7X_MXU_DIM) // head_dim
    head_block_ones = (blk_id[:, None] == blk_id[None, :]).astype(BF16)
    n_moe = moe_w_gate.shape[0]
    ne, ffe = d.n_experts, d.d_ff_expert
    flat = lambda w: w.reshape(-1, w.shape[-1])
    moe_w16 = {}

    xt = x.reshape(t_total, dm)
    for i in range(d.depth):
        j = i // 2
        mod = mod_all[i]
        if i % 2 == 0:
            ride = j < n_moe
            gate_up = Cast((flat(moe_w_gate), flat(moe_w_up)) if ride else (), j, n_moe)
            down = Cast((flat(moe_w_down),) if ride else (), j, n_moe)
            xt, *gu16 = _conv_sublayer(xt, mod, norm_g[i, 0], conv_w_pw1[j].astype(BF16), conv_b_pw1[j],
                                       conv_w_dw[j], conv_b_dw[j], conv_ln_g[j], conv_ln_b[j],
                                       conv_w_pw2[j].astype(BF16), conv_b_pw2[j], gate_up, d, plan)
            xt, *dn16 = _ffn_sublayer(xt, mod, norm_g[i, 1], ffn_w_gate[j].astype(BF16),
                                      ffn_w_up[j].astype(BF16), ffn_w_down[j].astype(BF16), down, d, plan)
            if ride:
                moe_w16[j] = (gu16[0].reshape(ne, dm, ffe), gu16[1].reshape(ne, dm, ffe),
                              dn16[0].reshape(ne, ffe, dm))
        else:
            wqkv = attn_w_qkv[j]
            nkv = n_kv * head_dim
            dup = lambda w: jnp.repeat(w.reshape(dm, n_kv, 1, head_dim), 2, axis=2).reshape(dm, 2 * nkv)
            w_qkvd = jnp.concatenate([wqkv[:, :nq], dup(wqkv[:, nq:nq + nkv]), dup(wqkv[:, nq + nkv:])],
                                     axis=1).astype(BF16)
            gain_row = jnp.concatenate([jnp.tile(attn_q_gain[j] * head_dim ** -0.5, n_heads),
                                        jnp.tile(attn_k_gain[j], 2 * n_kv)])[None]
            q, kd, vd = _qkv_project(xt, mod, norm_g[i, 0], w_qkvd, gain_row, head_block_ones, rope, d, plan)
            xt = _attention(q, kd, vd, attn_sinks[j], xt, mod, attn_w_o[j].astype(BF16), d, plan)
            xt = _moe_sublayer(xt, mod, norm_g[i, 1], moe_w_router[j], moe_b_router[j],
                               *moe_w16[j], d, plan)
    return xt.reshape(bsz, seq, dm)
```

```python
import functools
from typing import NamedTuple

import jax
import jax.numpy as jnp
from jax import lax
from jax.experimental import pallas as pl
from jax.experimental.pallas import tpu as pltpu

F32 = jnp.float32
BF16 = jnp.bfloat16
I32 = jnp.int32

V7X_LANES = 128
V7X_SUBLANES = 8
V7X_MXU_DIM = 256
ROW_ALIGN = 2 * V7X_SUBLANES
V7X_VMEM_LIMIT_BYTES = 56 * 1024 * 1024

ROPE_THETA = 500000.0
EPS = 1e-6
ATTN_BLOCK = 128
TOP_K = 2
MASK_VALUE = -1e30


class Dims(NamedTuple):
    batch: int
    seq: int
    d_model: int
    depth: int
    conv_kernel: int
    n_heads: int
    n_kv_heads: int
    head_dim: int
    d_ff_dense: int
    n_experts: int
    d_ff_expert: int


class Plan(NamedTuple):
    tm: int
    tm_attn: int
    tf_expert: int
    tg: int
    expert_sub: int
    proj_sub: int
    stage_sub: int
    tn_ada: int
    conv_rows: int
    conv_cols: int
    chunk: int


def _largest_tile(total, target, quantum):
    best = None
    t = quantum
    while t <= min(total, target):
        if total % t == 0:
            best = t
        t += quantum
    return best if best is not None else total


def _make_plan(d: Dims) -> Plan:
    tm = _largest_tile(d.seq, 512, ATTN_BLOCK)
    return Plan(
        tm=tm,
        tm_attn=_largest_tile(d.seq, 2 * tm, ATTN_BLOCK),
        tf_expert=_largest_tile(d.d_ff_expert, 1792, V7X_MXU_DIM),
        tg=_largest_tile(d.batch * d.seq, 1024, ATTN_BLOCK),
        expert_sub=2,
        proj_sub=4,
        stage_sub=1,
        tn_ada=_largest_tile(6 * d.d_model, 1536, V7X_LANES),
        conv_rows=128,
        conv_cols=V7X_LANES,
        chunk=ATTN_BLOCK // 2,
    )


def _params(*semantics):
    return pltpu.CompilerParams(dimension_semantics=semantics,
                                vmem_limit_bytes=V7X_VMEM_LIMIT_BYTES)


def _dot(a, b):
    return jnp.dot(a, b, preferred_element_type=F32)


def _sigmoid(x):
    return 1.0 / (1.0 + jnp.exp(-x))


def _norm_mod(x, g_row, shift_row, scale_row):
    y = x * lax.rsqrt(jnp.mean(x * x, axis=-1, keepdims=True) + EPS)
    return (y * g_row) * (1.0 + scale_row) + shift_row


def _full(shape):
    return pl.BlockSpec(shape, lambda *_: (0,) * len(shape))


def _ada_body(c_ref, w_ref, b_ref, o_ref):
    c = c_ref[...]
    ca = c * _sigmoid(c)
    o_ref[0] = jnp.dot(ca, w_ref[0], preferred_element_type=F32,
                       precision=lax.Precision.HIGHEST) + b_ref[0]


def _ada_all_layers(c, w_ada, b_ada, plan):
    depth, dm, n6 = w_ada.shape
    bsz = c.shape[0]
    tn = plan.tn_ada
    return pl.pallas_call(
        _ada_body,
        grid=(depth, n6 // tn),
        in_specs=[_full((bsz, dm)),
                  pl.BlockSpec((1, dm, tn), lambda i, j: (i, 0, j)),
                  pl.BlockSpec((1, 1, tn), lambda i, j: (i, 0, j))],
        out_specs=pl.BlockSpec((1, bsz, tn), lambda i, j: (i, 0, j)),
        out_shape=jax.ShapeDtypeStruct((depth, bsz, n6), F32),
        compiler_params=_params("arbitrary", "arbitrary"),
        name="adaln_mod",
    )(c, w_ada, b_ada.reshape(depth, 1, n6))


def _rope_body(pos_ref, f_ref, c_ref, s_ref):
    ang = pos_ref[...].astype(F32) * f_ref[...]
    c_ref[...] = jnp.cos(ang)
    s_ref[...] = jnp.sin(ang)


def _rope_tables(positions, d: Dims, plan):
    t_total = d.batch * d.seq
    rot = d.head_dim // 4
    half = rot // 2
    per_row = V7X_LANES // half
    n_rows = t_total // per_row
    assert n_rows * per_row == t_total and n_rows % V7X_SUBLANES == 0
    inv_freq = ROPE_THETA ** (-jnp.arange(0, rot, 2, dtype=F32) / rot)
    pos = jnp.repeat(positions.reshape(n_rows, per_row), half, axis=1)
    tab = jax.ShapeDtypeStruct((n_rows, V7X_LANES), F32)
    cos, sin = pl.pallas_call(
        _rope_body,
        in_specs=[_full((n_rows, V7X_LANES)), _full((1, V7X_LANES))],
        out_specs=[_full((n_rows, V7X_LANES)), _full((n_rows, V7X_LANES))],
        out_shape=[tab, tab],
        grid=(1,),
        compiler_params=_params("arbitrary"),
        name="rope_tables",
    )(pos, jnp.tile(inv_freq, per_row)[None])
    cos, sin = cos.reshape(t_total, half), sin.reshape(t_total, half)
    rest = d.head_dim - rot
    zeros = jnp.zeros((t_total, half), F32)
    pad0 = jnp.zeros((t_total, rest), F32)
    heads = V7X_LANES // d.head_dim
    c_tab = jnp.tile(jnp.concatenate([cos, cos, jnp.ones((t_total, rest), F32)], axis=1), (1, heads))
    s_hi = jnp.tile(jnp.concatenate([zeros, sin, pad0], axis=1), (1, heads))
    s_lo = jnp.tile(jnp.concatenate([-sin, zeros, pad0], axis=1), (1, heads))
    return c_tab, s_hi, s_lo


def _cast_riders(refs, n_cast):
    for src, dst in zip(refs[:n_cast], refs[len(refs) - n_cast:]):
        dst[...] = src[...].astype(dst.dtype)


class Cast(NamedTuple):
    arrays: tuple
    layer: int
    n_layers: int


def _rider_specs(cast: Cast, n_steps):
    in_specs, out_specs, shapes = [], [], []
    for a in cast.arrays:
        layer_rows = a.shape[0] // cast.n_layers
        rows = layer_rows // n_steps
        assert rows * n_steps * cast.n_layers == a.shape[0] and rows % ROW_ALIGN == 0
        first = cast.layer * n_steps
        in_specs.append(pl.BlockSpec((rows, a.shape[1]), lambda i, first=first: (first + i, 0)))
        out_specs.append(pl.BlockSpec((rows, a.shape[1]), lambda i: (i, 0)))
        shapes.append(jax.ShapeDtypeStruct((layer_rows, a.shape[1]), BF16))
    return in_specs, out_specs, shapes


def _conv_body(x_ref, mod_ref, ng_ref, w1_ref, b1_ref, wd_ref, bd_ref, lg_ref, lb_ref,
               w2_ref, b2_ref, *rest, n_cast, dm, ktaps, halo, tiles_per_seq, rc, cb, n_sub):
    ubuf, wpk, vbuf = rest[-3:]
    o_ref = rest[n_cast]
    _cast_riders(rest[:-3], n_cast)
    m = pl.program_id(0)
    tm = x_ref.shape[0]
    mod = mod_ref[0]
    half = dm // 2
    high16 = jnp.uint32(0xFFFF0000)

    def bf16_bits(a):
        return lax.bitcast_convert_type(a.astype(BF16).astype(F32), jnp.uint32)

    def pack2(a):
        return bf16_bits(a[:, half:]) | (bf16_bits(a[:, :half]) >> 16)

    @pl.when(m % tiles_per_seq == 0)
    def _():
        ubuf[0:halo, :] = jnp.zeros((halo, half), ubuf.dtype)

    sub = tm // n_sub
    for sb in range(n_sub):
        rows = slice(sb * sub, (sb + 1) * sub)
        h = _norm_mod(x_ref[rows, :], ng_ref[...], mod[0:1], mod[1:2]).astype(BF16)
        u = _dot(h, w1_ref[...]) + b1_ref[...]
        ubuf[halo + sb * sub:halo + (sb + 1) * sub, :] = pack2(u[:, :dm] * _sigmoid(u[:, dm:]))
    wpk[...] = pack2(wd_ref[...])

    def chunk(ci, carry):
        r0 = pl.multiple_of(ci * rc, rc)
        lo, hi = [], []
        for cbi in range(half // cb):
            cols = slice(cbi * cb, (cbi + 1) * cb)
            win = pltpu.bitcast(ubuf[pl.ds(r0, rc + halo), cols], BF16)
            acc = None
            for r in range(V7X_SUBLANES):
                z = None
                for q in range(halo // V7X_SUBLANES):
                    dl = V7X_SUBLANES * q + r
                    if dl >= ktaps:
                        continue
                    s = halo - V7X_SUBLANES * (q + 1)
                    taps = jnp.broadcast_to(wpk[dl:dl + 1, cols], (rc + V7X_SUBLANES, cb))
                    term = pltpu.bitcast(taps, BF16) * win[2 * s:2 * (s + rc + V7X_SUBLANES), :]
                    z = term if z is None else z + term
                if z is None:
                    continue
                z = pltpu.bitcast(z, ubuf.dtype)
                if r:
                    z = pltpu.roll(z, r, 0)
                z = pltpu.bitcast(z[V7X_SUBLANES:V7X_SUBLANES + rc, :], BF16)
                acc = z if acc is None else acc + z
            acc = pltpu.bitcast(acc, ubuf.dtype)
            lo.append(lax.bitcast_convert_type(acc << 16, F32))
            hi.append(lax.bitcast_convert_type(acc & high16, F32))
        conv = jnp.concatenate(lo + hi, axis=1) + bd_ref[...]
        mu = jnp.mean(conv, axis=-1, keepdims=True)
        cen = conv - mu
        var = jnp.mean(cen * cen, axis=-1, keepdims=True)
        y = cen * lax.rsqrt(var + EPS) * lg_ref[...] + lb_ref[...]
        vbuf[pl.ds(r0, rc), :] = (y * _sigmoid(y)).astype(BF16)
        return carry

    lax.fori_loop(0, tm // rc, chunk, 0)
    ubuf[0:halo, :] = ubuf[tm:tm + halo, :]
    y2 = _dot(vbuf[...], w2_ref[...]) + b2_ref[...]
    o_ref[...] = x_ref[...] + mod[2:3] * y2


def _conv_sublayer(x, mod, ng, w1, b1, w_dw, b_dw, ln_g, ln_b, w2, b2, cast, d: Dims, plan):
    t_total, dm = x.shape
    tm = plan.tm
    n_steps = t_total // tm
    ktaps = w_dw.shape[0]
    halo = -(-ktaps // V7X_SUBLANES) * V7X_SUBLANES
    wd = jnp.zeros((halo, dm), F32).at[:ktaps].set(w_dw[::-1])
    tiles_per_seq = d.seq // tm
    row = pl.BlockSpec((tm, dm), lambda i: (i, 0))
    cast_in, cast_out, cast_shapes = _rider_specs(cast, n_steps)
    body = functools.partial(_conv_body, n_cast=len(cast.arrays), dm=dm, ktaps=ktaps, halo=halo,
                             tiles_per_seq=tiles_per_seq, rc=plan.conv_rows, cb=plan.conv_cols,
                             n_sub=1)
    return pl.pallas_call(
        body,
        grid=(n_steps,),
        in_specs=[row,
                  pl.BlockSpec((1, 6, dm), lambda i: (i // tiles_per_seq, 0, 0)),
                  _full((1, dm)), _full((dm, 2 * dm)), _full((1, 2 * dm)),
                  _full(wd.shape), _full((1, dm)), _full((1, dm)), _full((1, dm)),
                  _full((dm, dm)), _full((1, dm))] + cast_in,
        out_specs=[row] + cast_out,
        out_shape=[jax.ShapeDtypeStruct((t_total, dm), F32)] + cast_shapes,
        scratch_shapes=[pltpu.VMEM((halo + tm, dm // 2), jnp.uint32),
                        pltpu.VMEM((halo, dm // 2), jnp.uint32), pltpu.VMEM((tm, dm), BF16)],
        compiler_params=_params("arbitrary"),
        name="conv_sublayer",
    )(x, mod, ng[None], w1, b1[None], wd, b_dw[None], ln_g[None], ln_b[None], w2, b2[None],
      *cast.arrays)


def _swiglu(h, wg, wu, wd):
    g = _dot(h, wg)
    u = _dot(h, wu)
    return _dot(((g * _sigmoid(g)) * u).astype(BF16), wd)


def _ffn_body(x_ref, mod_ref, ng_ref, wg_ref, wu_ref, wd_ref, *rest, n_cast):
    o_ref = rest[n_cast]
    _cast_riders(rest, n_cast)
    mod = mod_ref[0]
    x = x_ref[...]
    h = _norm_mod(x, ng_ref[...], mod[3:4], mod[4:5]).astype(BF16)
    o_ref[...] = x + mod[5:6] * _swiglu(h, wg_ref[...], wu_ref[...], wd_ref[...])


def _ffn_sublayer(x, mod, ng, wg, wu, wd, cast, d: Dims, plan):
    t_total, dm = x.shape
    tm = plan.tm
    n_steps = t_total // tm
    tiles_per_seq = d.seq // tm
    row = pl.BlockSpec((tm, dm), lambda i: (i, 0))
    resident = pl.BlockSpec(memory_space=pltpu.VMEM)
    cast_in, cast_out, cast_shapes = _rider_specs(cast, n_steps)
    return pl.pallas_call(
        functools.partial(_ffn_body, n_cast=len(cast.arrays)),
        grid=(n_steps,),
        in_specs=[row,
                  pl.BlockSpec((1, 6, dm), lambda i: (i // tiles_per_seq, 0, 0)),
                  _full((1, dm)), resident, resident, resident] + cast_in,
        out_specs=[row] + cast_out,
        out_shape=[jax.ShapeDtypeStruct((t_total, dm), F32)] + cast_shapes,
        compiler_params=_params("arbitrary"),
        name="dense_swiglu",
    )(x, mod, ng[None], wg, wu, wd, *cast.arrays)


def _expert_body(te_ref, nv_ref, xs_ref, wg_ref, wu_ref, wd_ref, ys_ref, acc_ref, *, n_sub):
    i = pl.program_id(0)
    f = pl.program_id(1)
    rows = xs_ref.shape[0] // n_sub

    @pl.when(i < nv_ref[0])
    def _():
        @pl.when(f == 0)
        def _():
            acc_ref[...] = jnp.zeros_like(acc_ref)

        for s in range(n_sub):
            sl = slice(s * rows, (s + 1) * rows)
            acc_ref[sl, :] += _swiglu(xs_ref[sl, :], wg_ref[0], wu_ref[0], wd_ref[0])

        @pl.when(f == pl.num_programs(1) - 1)
        def _():
            ys_ref[...] = acc_ref[...].astype(ys_ref.dtype)

    @pl.when((i >= nv_ref[0]) & (f == 0))
    def _():
        ys_ref[...] = jnp.zeros_like(ys_ref)


def _expert_swiglu(xs, tile_expert, n_valid, n_tiles, wg, wu, wd, plan):
    dm = xs.shape[1]
    ff = wg.shape[2]
    tg, tf = plan.tg, plan.tf_expert
    p_rows = n_tiles * tg
    nf = ff // tf

    def row_map(i, f, te, nv):
        return (jnp.minimum(i, nv[0] - 1), 0)

    def fsel(i, f, nv):
        return jnp.where(i < nv[0], f, nf - 1)

    grid_spec = pltpu.PrefetchScalarGridSpec(
        num_scalar_prefetch=2,
        grid=(p_rows // tg, nf),
        in_specs=[pl.BlockSpec((tg, dm), row_map),
                  pl.BlockSpec((1, dm, tf), lambda i, f, te, nv: (te[i], 0, fsel(i, f, nv))),
                  pl.BlockSpec((1, dm, tf), lambda i, f, te, nv: (te[i], 0, fsel(i, f, nv))),
                  pl.BlockSpec((1, tf, dm), lambda i, f, te, nv: (te[i], fsel(i, f, nv), 0))],
        out_specs=pl.BlockSpec((tg, dm), lambda i, f, te, nv: (i, 0)),
        scratch_shapes=[pltpu.VMEM((tg, dm), F32)],
    )
    return pl.pallas_call(
        functools.partial(_expert_body, n_sub=plan.expert_sub),
        grid_spec=grid_spec,
        out_shape=jax.ShapeDtypeStruct((p_rows, dm), BF16),
        compiler_params=_params("arbitrary", "arbitrary"),
        name="expert_swiglu",
    )(tile_expert, n_valid, xs, wg, wu, wd)


def _qkv_body(x_ref, mod_ref, ng_ref, w_ref, gain_ref, bd_ref, c_ref, s1_ref, s2_ref,
              q_ref, k_ref, v_ref, *, nq, nk, head_dim, half_rot, n_sub):
    mod = mod_ref[0]
    bd = bd_ref[...]
    sub = x_ref.shape[0] // n_sub
    for sb in range(n_sub):
        rows = slice(sb * sub, (sb + 1) * sub)
        h = _norm_mod(x_ref[rows, :], ng_ref[...], mod[0:1], mod[1:2]).astype(BF16)
        qkv = _dot(h, w_ref[...])
        v_ref[rows, :] = qkv[:, nq + nk:].astype(BF16)
        cos = c_ref[rows, :]
        sin_hi = s1_ref[rows, :]
        sin_lo = s2_ref[rows, :]
        for j in range((nq + nk) // V7X_MXU_DIM):
            blk = qkv[:, j * V7X_MXU_DIM:(j + 1) * V7X_MXU_DIM]
            ss = _dot((blk * blk).astype(BF16), bd)
            yn = ((blk * lax.rsqrt(ss * (1.0 / head_dim) + EPS))
                  * gain_ref[:, j * V7X_MXU_DIM:(j + 1) * V7X_MXU_DIM])
            for p in range(V7X_MXU_DIM // V7X_LANES):
                xx = yn[:, p * V7X_LANES:(p + 1) * V7X_LANES]
                rot = (xx * cos + pltpu.roll(xx, half_rot, 1) * sin_hi
                       + pltpu.roll(xx, V7X_LANES - half_rot, 1) * sin_lo).astype(BF16)
                col = j * V7X_MXU_DIM + p * V7X_LANES
                if col < nq:
                    q_ref[rows, col:col + V7X_LANES] = rot
                else:
                    k_ref[rows, col - nq:col - nq + V7X_LANES] = rot


def _qkv_project(x, mod, ng, w_qkvd, gain_row, bd, rope, d: Dims, plan):
    t_total, dm = x.shape
    tm = plan.tm_attn
    nq = d.n_heads * d.head_dim
    nk = d.n_kv_heads * 2 * d.head_dim
    tiles_per_seq = d.seq // tm
    row = pl.BlockSpec((tm, dm), lambda i: (i, 0))
    tab = pl.BlockSpec((tm, V7X_LANES), lambda i: (i, 0))
    body = functools.partial(_qkv_body, nq=nq, nk=nk, head_dim=d.head_dim,
                             half_rot=d.head_dim // 8, n_sub=plan.proj_sub * (tm // plan.tm))
    return pl.pallas_call(
        body,
        grid=(t_total // tm,),
        in_specs=[row,
                  pl.BlockSpec((1, 6, dm), lambda i: (i // tiles_per_seq, 0, 0)),
                  _full((1, dm)), _full((dm, nq + 2 * nk)), _full((1, nq + nk)),
                  _full((V7X_MXU_DIM, V7X_MXU_DIM)), tab, tab, tab],
        out_specs=[pl.BlockSpec((tm, nq), lambda i: (i, 0)),
                   pl.BlockSpec((tm, nk), lambda i: (i, 0)),
                   pl.BlockSpec((tm, nk), lambda i: (i, 0))],
        out_shape=[jax.ShapeDtypeStruct((t_total, nq), BF16),
                   jax.ShapeDtypeStruct((t_total, nk), BF16),
                   jax.ShapeDtypeStruct((t_total, nk), BF16)],
        compiler_params=_params("arbitrary"),
        name="qkv_project",
    )(x, mod, ng[None], w_qkvd, gain_row, bd, *rope)


def _attn_body(q_ref, k_ref, v_ref, kp_ref, vp_ref, sink_ref, x_ref, mod_ref, wo_ref, o_ref,
               o_scr, *, n_kv, group, head_dim, tiles_per_seq):
    m = pl.program_id(0)
    tq = q_ref.shape[0]
    blk = ATTN_BLOCK
    qi = lax.broadcasted_iota(I32, (blk, blk), 0)
    kj = lax.broadcasted_iota(I32, (blk, blk), 1)
    from_prev = kj > qi
    kmin = jnp.where(m % tiles_per_seq == 0, blk, 0)
    lo = lax.broadcasted_iota(I32, (blk, V7X_LANES), 1) < head_dim
    zero = jnp.zeros((blk, V7X_LANES), BF16)
    pairs = group // 2
    for n in range(tq // blk):
        rows = slice(n * blk, (n + 1) * blk)
        if n == 0:
            prev_ok = from_prev & (kj >= kmin)
            kprev, vprev = kp_ref[...], vp_ref[...]
        else:
            prev_ok = None
            prev = slice((n - 1) * blk, n * blk)
            kprev, vprev = k_ref[prev, :], v_ref[prev, :]
        kcur, vcur = k_ref[rows, :], v_ref[rows, :]
        for g in range(n_kv):
            gl = slice(g * V7X_LANES, (g + 1) * V7X_LANES)
            kk = jnp.concatenate([kprev[:, gl], kcur[:, gl]], axis=0)
            vv = jnp.concatenate([vprev[:, gl], vcur[:, gl]], axis=0)
            slabs = []
            for pr in range(pairs):
                c0 = (g * pairs + pr) * V7X_LANES
                qp = q_ref[rows, c0:c0 + V7X_LANES]
                slabs += [jnp.where(lo, qp, zero), jnp.where(lo, zero, qp)]
            qs = jnp.concatenate(slabs, axis=0)
            s = lax.dot_general(qs, kk, (((1,), (1,)), ((), ())), preferred_element_type=F32)
            ps, dens = [], []
            for hh in range(group):
                s_prev = s[hh * blk:(hh + 1) * blk, :blk]
                s_cur = s[hh * blk:(hh + 1) * blk, blk:]
                if prev_ok is None:
                    sh = jnp.where(from_prev, s_prev, s_cur)
                else:
                    sh = jnp.where(prev_ok, s_prev, jnp.where(from_prev, MASK_VALUE, s_cur))
                sink = sink_ref[0, g * group + hh]
                mx = jnp.maximum(jnp.max(sh, axis=-1, keepdims=True), sink)
                p = jnp.exp(sh - mx)
                dens.append(jnp.sum(p, axis=-1, keepdims=True) + jnp.exp(sink - mx))
                ps.append(jnp.concatenate([jnp.where(from_prev, p, 0.0), jnp.where(from_prev, 0.0, p)],
                                          axis=1).astype(BF16))
            o = _dot(jnp.concatenate(ps, axis=0), vv)
            for pr in range(pairs):
                h0, h1 = 2 * pr, 2 * pr + 1
                oa = o[h0 * blk:(h0 + 1) * blk, :] / dens[h0]
                ob = o[h1 * blk:(h1 + 1) * blk, :] / dens[h1]
                c0 = (g * pairs + pr) * V7X_LANES
                o_scr[rows, c0:c0 + V7X_LANES] = jnp.where(lo, oa, ob).astype(BF16)
    mod = mod_ref[0]
    o_ref[...] = x_ref[...] + mod[2:3] * _dot(o_scr[...], wo_ref[...])


def _attention(q, kd, vd, sinks, x, mod, wo, d: Dims, plan):
    t_total, dm = x.shape
    tq = plan.tm_attn
    nq = q.shape[1]
    nk = kd.shape[1]
    tiles_per_seq = d.seq // tq
    bpt = tq // ATTN_BLOCK
    group = d.n_heads // d.n_kv_heads
    prev = pl.BlockSpec((ATTN_BLOCK, nk), lambda i: (jnp.maximum(i * bpt - 1, 0), 0))
    cur = pl.BlockSpec((tq, nk), lambda i: (i, 0))
    body = functools.partial(_attn_body, n_kv=d.n_kv_heads, group=group, head_dim=d.head_dim,
                             tiles_per_seq=tiles_per_seq)
    return pl.pallas_call(
        body,
        grid=(t_total // tq,),
        in_specs=[pl.BlockSpec((tq, nq), lambda i: (i, 0)), cur, cur, prev, prev,
                  pl.BlockSpec(memory_space=pltpu.SMEM),
                  pl.BlockSpec((tq, dm), lambda i: (i, 0)),
                  pl.BlockSpec((1, 6, dm), lambda i: (i // tiles_per_seq, 0, 0)),
                  _full((nq, dm))],
        out_specs=pl.BlockSpec((tq, dm), lambda i: (i, 0)),
        out_shape=jax.ShapeDtypeStruct((t_total, dm), F32),
        scratch_shapes=[pltpu.VMEM((tq, nq), BF16)],
        compiler_params=_params("arbitrary"),
        name="swa_attention",
    )(q, kd, vd, kd, vd, sinks[None], x, mod, wo)


def _router_body(x_ref, mod_ref, ng_ref, wr_ref, br_ref, ltri_ref, h_ref, meta_ref, cnt_ref,
                 *, n_experts):
    tt = x_ref.shape[0]
    mod = mod_ref[0]
    h2 = _norm_mod(x_ref[...], ng_ref[...], mod[3:4], mod[4:5])
    h_hi = h2.astype(BF16)
    h_ref[...] = h_hi
    h_lo = (h2 - h_hi.astype(F32)).astype(BF16)
    both = _dot(h_hi, wr_ref[...])
    logits = (both[:, :V7X_LANES] + both[:, V7X_LANES:]
              + _dot(h_lo, wr_ref[:, :V7X_LANES])) + br_ref[...]
    lane = lax.broadcasted_iota(I32, (tt, V7X_LANES), 1).astype(F32)
    neg = jnp.float32(-jnp.inf)
    logits = jnp.where(lane < n_experts, logits, neg)
    big = jnp.float32(V7X_LANES)
    m1 = jnp.max(logits, axis=-1, keepdims=True)
    i1 = jnp.min(jnp.where(logits == m1, lane, big), axis=-1, keepdims=True)
    rest = jnp.where(lane == i1, neg, logits)
    m2 = jnp.max(rest, axis=-1, keepdims=True)
    i2 = jnp.min(jnp.where(rest == m2, lane, big), axis=-1, keepdims=True)
    e2 = jnp.exp(m2 - m1)
    w1 = 1.0 / (1.0 + e2)
    w2 = e2 / (1.0 + e2)

    sel1 = lane == i1
    sel2 = lane == i2
    assigned = jnp.where(sel1 | sel2, 1.0, 0.0)
    before = _dot(ltri_ref[...], assigned.astype(BF16))
    r1 = jnp.sum(jnp.where(sel1, before, 0.0), axis=-1, keepdims=True)
    r2 = jnp.sum(jnp.where(sel2, before, 0.0), axis=-1, keepdims=True)
    cnt_ref[0] = jnp.broadcast_to(jnp.sum(assigned, axis=0, keepdims=True), cnt_ref.shape[1:])
    meta = jnp.zeros((tt, V7X_LANES), F32)
    for col, val in enumerate((i1, i2, r1, r2, w1, w2)):
        meta = jnp.where(lane == col, val, meta)
    meta_ref[...] = meta


META_EXPERT, META_RANK, META_GATE = 0, 2, 4


def _route(x, mod, ng, w_router, b_router, d: Dims, plan):
    t_total, dm = x.shape
    tt = plan.tm
    ne = d.n_experts
    n_rt = t_total // tt
    tiles_per_seq = d.seq // tt
    wr = jnp.zeros((dm, V7X_LANES), F32).at[:, :ne].set(w_router)
    wr_hi = wr.astype(BF16)
    wr = jnp.concatenate([wr_hi, (wr - wr_hi.astype(F32)).astype(BF16)], axis=1)
    br = jnp.zeros((1, V7X_LANES), F32).at[0, :ne].set(b_router)
    ltri = jnp.tril(jnp.ones((tt, tt), BF16), -1)
    row = pl.BlockSpec((tt, dm), lambda i: (i, 0))
    meta = pl.BlockSpec((tt, V7X_LANES), lambda i: (i, 0))
    return pl.pallas_call(
        functools.partial(_router_body, n_experts=ne),
        grid=(n_rt,),
        in_specs=[row,
                  pl.BlockSpec((1, 6, dm), lambda i: (i // tiles_per_seq, 0, 0)),
                  _full((1, dm)), _full((dm, 2 * V7X_LANES)), _full((1, V7X_LANES)), _full((tt, tt))],
        out_specs=[row, meta, pl.BlockSpec((1, V7X_SUBLANES, V7X_LANES), lambda i: (i, 0, 0))],
        out_shape=[jax.ShapeDtypeStruct((t_total, dm), BF16),
                   jax.ShapeDtypeStruct((t_total, V7X_LANES), F32),
                   jax.ShapeDtypeStruct((n_rt, V7X_SUBLANES, V7X_LANES), F32)],
        compiler_params=_params("arbitrary"),
        name="moe_router",
    )(x, mod, ng[None], wr, br, ltri)


def _chunks(count, ch):
    return lax.shift_right_logical(count + (ch - 1), jnp.int32(ch.bit_length() - 1))


def _stage_offsets(cnt_ref, base, n_experts, ch):
    offs = []
    total = jnp.int32(0)
    for e in range(n_experts):
        offs.append(total)
        total = total + _chunks(cnt_ref[base + e], ch) * ch
    return offs, total


def _stage_pos(meta, lanes, offs):
    off_row = jnp.zeros((1, lanes.shape[1]), F32)
    for e, off in enumerate(offs):
        off_row = jnp.where(lanes[0:1, :] == e, off.astype(F32), off_row)
    pos = []
    for k in range(TOP_K):
        expert = meta[:, META_EXPERT + k:META_EXPERT + k + 1]
        seg_off = jnp.sum(jnp.where(lanes == expert, off_row, 0.0), axis=-1, keepdims=True)
        pos.append(seg_off + meta[:, META_RANK + k:META_RANK + k + 1])
    return pos


def _dispatch_body(seg_ref, cnt_ref, zf_ref, nv_ref, h_ref, meta_ref, xs_hbm, stage, sem,
                   *, n_experts, ch, tg, n_fill_tiles, n_sub):
    m = pl.program_id(0)
    tt = h_ref.shape[0]
    buf = m % 2

    def block_copy(b, src_row, dst_row, rows):
        return pltpu.make_async_copy(stage.at[b, pl.ds(src_row, rows), :],
                                     xs_hbm.at[pl.ds(dst_row, rows), :], sem.at[b])

    def drain(b, n_chunks):
        def one(i, c):
            block_copy(b, 0, 0, ch).wait()
            return c

        lax.fori_loop(0, n_chunks, one, 0)

    @pl.when(m == 0)
    def _():
        stage[0, 0:tg + ch, :] = jnp.zeros((tg + ch, stage.shape[2]), stage.dtype)
        for e in range(n_experts):
            cp = block_copy(0, 0, pl.multiple_of(zf_ref[e], ROW_ALIGN), tg + ch)
            cp.start()
            cp.wait()

        def fill(j, c):
            cp = block_copy(0, 0, pl.multiple_of(j * tg, tg), tg)
            cp.start()
            cp.wait()
            return c

        lax.fori_loop(nv_ref[0], n_fill_tiles, fill, 0)

    offs, total = _stage_offsets(cnt_ref, m * n_experts, n_experts, ch)
    lanes = lax.broadcasted_iota(I32, (tt, V7X_LANES), 1).astype(F32)
    pos = _stage_pos(meta_ref[...], lanes, offs)
    cols = jnp.full((tt, V7X_LANES), -1.0, F32)
    for k in range(TOP_K):
        cols = jnp.where(lanes == k, pos[k], cols)
    pos_rows = jnp.transpose(cols)
    sub = stage.shape[1] // n_sub
    for sb in range(n_sub):
        slot = (lax.broadcasted_iota(I32, (sub, tt), 0) + sb * sub).astype(F32)
        onehot = jnp.zeros(slot.shape, F32)
        for k in range(TOP_K):
            onehot = jnp.where(slot == pos_rows[k:k + 1, :], 1.0, onehot)
        stage[buf, sb * sub:(sb + 1) * sub, :] = _dot(onehot.astype(BF16), h_ref[...]).astype(stage.dtype)

    @pl.when(m > 0)
    def _():
        _, prev_total = _stage_offsets(cnt_ref, jnp.maximum(m - 1, 0) * n_experts, n_experts, ch)
        drain(1 - buf, _chunks(prev_total, ch))

    for e in range(n_experts):
        start = seg_ref[m * n_experts + e]

        def send(c, carry, start=start, off=offs[e]):
            block_copy(buf, pl.multiple_of(off + c * ch, ch),
                       pl.multiple_of(start + c * ch, ROW_ALIGN), ch).start()
            return carry

        lax.fori_loop(0, _chunks(cnt_ref[m * n_experts + e], ch), send, 0)

    @pl.when(m == pl.num_programs(0) - 1)
    def _():
        drain(buf, _chunks(total, ch))


def _dispatch(h2, meta, seg_start, seg_cnt, zfill_start, n_valid, n_rows, d: Dims, plan):
    t_total, dm = h2.shape
    tt, tg, ch = plan.tm, plan.tg, plan.chunk
    ne = d.n_experts
    grid_spec = pltpu.PrefetchScalarGridSpec(
        num_scalar_prefetch=4,
        grid=(t_total // tt,),
        in_specs=[pl.BlockSpec((tt, dm), lambda i, *_: (i, 0)),
                  pl.BlockSpec((tt, V7X_LANES), lambda i, *_: (i, 0))],
        out_specs=pl.BlockSpec(memory_space=pl.ANY),
        scratch_shapes=[pltpu.VMEM((2, max(TOP_K * tt + ne * ch, tg + ch), dm), BF16),
                        pltpu.SemaphoreType.DMA((2,))],
    )
    body = functools.partial(_dispatch_body, n_experts=ne, ch=ch, tg=tg, n_fill_tiles=n_rows // tg,
                             n_sub=plan.stage_sub)
    return pl.pallas_call(
        body,
        grid_spec=grid_spec,
        out_shape=jax.ShapeDtypeStruct((n_rows, dm), BF16),
        compiler_params=_params("arbitrary"),
        name="moe_dispatch",
    )(seg_start, seg_cnt, zfill_start, n_valid, h2, meta)


def _combine_body(seg_ref, cnt_ref, ys_hbm, x_ref, mod_ref, meta_ref, o_ref, stage, sem,
                  *, n_experts, ch, n_sub):
    m = pl.program_id(0)
    tt = x_ref.shape[0]
    buf = m % 2

    def fetch(b, src_row, dst_row):
        return pltpu.make_async_copy(ys_hbm.at[pl.ds(src_row, ch), :],
                                     stage.at[b, pl.ds(dst_row, ch), :], sem.at[b])

    def fetch_tile(tile, b):
        offs, _ = _stage_offsets(cnt_ref, tile * n_experts, n_experts, ch)
        for e in range(n_experts):
            start = seg_ref[tile * n_experts + e]

            def issue(c, carry, start=start, off=offs[e]):
                fetch(b, pl.multiple_of(start + c * ch, ROW_ALIGN),
                      pl.multiple_of(off + c * ch, ch)).start()
                return carry

            lax.fori_loop(0, _chunks(cnt_ref[tile * n_experts + e], ch), issue, 0)

    @pl.when(m == 0)
    def _():
        stage[...] = jnp.zeros_like(stage)
        fetch_tile(m, buf)

    @pl.when(m + 1 < pl.num_programs(0))
    def _():
        fetch_tile(m + 1, 1 - buf)

    offs, total = _stage_offsets(cnt_ref, m * n_experts, n_experts, ch)
    meta = meta_ref[...]
    lanes = lax.broadcasted_iota(I32, (tt, V7X_LANES), 1).astype(F32)
    pos = _stage_pos(meta, lanes, offs)
    def drain(i, c):
        fetch(buf, 0, 0).wait()
        return c

    lax.fori_loop(0, _chunks(total, ch), drain, 0)
    sub = tt // n_sub
    slot = lax.broadcasted_iota(I32, (sub, stage.shape[1]), 1).astype(F32)
    for sb in range(n_sub):
        rows = slice(sb * sub, (sb + 1) * sub)
        weights = jnp.zeros(slot.shape, F32)
        for k in range(TOP_K):
            weights = jnp.where(slot == pos[k][rows, :], meta[rows, META_GATE + k:META_GATE + k + 1], weights)
        y = _dot(weights.astype(BF16), stage[buf])
        o_ref[rows, :] = x_ref[rows, :] + mod_ref[0][5:6] * y


def _combine(ys, seg_start, seg_cnt, x, mod, meta, d: Dims, plan):
    t_total, dm = x.shape
    tt, ch = plan.tm, plan.chunk
    ne = d.n_experts
    tiles_per_seq = d.seq // tt
    row = pl.BlockSpec((tt, dm), lambda i, *_: (i, 0))
    grid_spec = pltpu.PrefetchScalarGridSpec(
        num_scalar_prefetch=2,
        grid=(t_total // tt,),
        in_specs=[pl.BlockSpec(memory_space=pl.ANY), row,
                  pl.BlockSpec((1, 6, dm), lambda i, *_: (i // tiles_per_seq, 0, 0)),
                  pl.BlockSpec((tt, V7X_LANES), lambda i, *_: (i, 0))],
        out_specs=row,
        scratch_shapes=[pltpu.VMEM((2, TOP_K * tt + ne * ch, dm), BF16),
                        pltpu.SemaphoreType.DMA((2,))],
    )
    return pl.pallas_call(
        functools.partial(_combine_body, n_experts=ne, ch=ch, n_sub=plan.proj_sub),
        grid_spec=grid_spec,
        out_shape=jax.ShapeDtypeStruct((t_total, dm), F32),
        compiler_params=_params("arbitrary"),
        name="moe_combine",
    )(seg_start, seg_cnt, ys, x, mod, meta)


def _round_up(v, q):
    return (v + q - 1) // q * q


def _moe_sublayer(x, mod, ng, w_router, b_router, wg, wu, wd, d: Dims, plan):
    t_total, dm = x.shape
    ne, tg, ch, tt = d.n_experts, plan.tg, plan.chunk, plan.tm
    n_rt = t_total // tt
    h2, meta, cnt = _route(x, mod, ng, w_router, b_router, d, plan)
    seg_cnt = cnt[:, 0, :ne].astype(I32)
    seg_len = _round_up(seg_cnt, ROW_ALIGN)
    used = jnp.sum(seg_len, axis=0)
    region = _round_up(used + ch, tg)
    region_end = jnp.cumsum(region)
    region_start = region_end - region
    seg_start = region_start[None, :] + jnp.cumsum(seg_len, axis=0) - seg_len
    n_valid = (region_end[ne - 1] // tg).astype(I32)
    n_tiles = -(-(TOP_K * t_total + n_rt * ne * (ROW_ALIGN - 1) + ne * ch) // tg) + ne
    tile_id = jnp.minimum(jnp.arange(n_tiles, dtype=I32), n_valid - 1)
    tile_expert = jnp.sum(tile_id[:, None] >= (region_end // tg)[None, :ne - 1], axis=1).astype(I32)
    seg_start = seg_start.reshape(-1).astype(I32)
    seg_cnt = seg_cnt.reshape(-1)
    xs = _dispatch(h2, meta, seg_start, seg_cnt, (region_start + used).astype(I32), n_valid.reshape(1),
                   (n_tiles + 2) * tg, d, plan)
    ys = _expert_swiglu(xs, tile_expert, n_valid.reshape(1), n_tiles, wg, wu, wd, plan)
    return _combine(ys, seg_start, seg_cnt, x, mod, meta, d, plan)


def kernel(x, c, positions, norm_g, w_ada, b_ada, conv_w_pw1, conv_b_pw1, conv_w_dw, conv_b_dw, conv_ln_g, conv_ln_b, conv_w_pw2, conv_b_pw2, attn_w_qkv, attn_q_gain, attn_k_gain, attn_sinks, attn_w_o, ffn_w_gate, ffn_w_up, ffn_w_down, moe_w_router, moe_b_router, moe_w_gate, moe_w_up, moe_w_down):
    bsz, seq, dm = x.shape
    head_dim = attn_q_gain.shape[1]
    n_heads = attn_sinks.shape[1]
    n_kv = (attn_w_qkv.shape[2] // head_dim - n_heads) // 2
    d = Dims(batch=bsz, seq=seq, d_model=dm, depth=norm_g.shape[0], conv_kernel=conv_w_dw.shape[1],
             n_heads=n_heads, n_kv_heads=n_kv, head_dim=head_dim, d_ff_dense=ffn_w_gate.shape[2],
             n_experts=moe_w_router.shape[2], d_ff_expert=moe_w_gate.shape[3])
    assert 2 * head_dim == V7X_LANES and (n_heads // n_kv) % 2 == 0 and n_heads % n_kv == 0
    assert seq % ATTN_BLOCK == 0 and dm % V7X_MXU_DIM == 0
    assert (n_heads + 2 * n_kv) * head_dim % V7X_MXU_DIM == 0 and n_kv * 2 * head_dim % V7X_MXU_DIM == 0
    plan = _make_plan(d)
    t_total = bsz * seq
    nq = n_heads * head_dim

    mod_all = _ada_all_layers(c, w_ada, b_ada, plan).reshape(d.depth, bsz, 6, dm)
    rope = _rope_tables(positions, d, plan)
    blk_id = jnp.arange(V7X_MXU_DIM) // head_dim
    head_block_ones = (blk_id[:, None] == blk_id[None, :]).astype(BF16)
    n_moe = moe_w_gate.shape[0]
    ne, ffe = d.n_experts, d.d_ff_expert
    flat = lambda w: w.reshape(-1, w.shape[-1])
    moe_w16 = {}

    xt = x.reshape(t_total, dm)
    for i in range(d.depth):
        j = i // 2
        mod = mod_all[i]
        if i % 2 == 0:
            ride = j < n_moe
            gate_up = Cast((flat(moe_w_gate), flat(moe_w_up)) if ride else (), j, n_moe)
            down = Cast((flat(moe_w_down),) if ride else (), j, n_moe)
            xt, *gu16 = _conv_sublayer(xt, mod, norm_g[i, 0], conv_w_pw1[j].astype(BF16), conv_b_pw1[j],
                                       conv_w_dw[j], conv_b_dw[j], conv_ln_g[j], conv_ln_b[j],
                                       conv_w_pw2[j].astype(BF16), conv_b_pw2[j], gate_up, d, plan)
            xt, *dn16 = _ffn_sublayer(xt, mod, norm_g[i, 1], ffn_w_gate[j].astype(BF16),
                                      ffn_w_up[j].astype(BF16), ffn_w_down[j].astype(BF16), down, d, plan)
            if ride:
                moe_w16[j] = (gu16[0].reshape(ne, dm, ffe), gu16[1].reshape(ne, dm, ffe),
                              dn16[0].reshape(ne, ffe, dm))
        else:
            wqkv = attn_w_qkv[j]
            nkv = n_kv * head_dim
            dup = lambda w: jnp.repeat(w.reshape(dm, n_kv, 1, head_dim), 2, axis=2).reshape(dm, 2 * nkv)
            w_qkvd = jnp.concatenate([wqkv[:, :nq], dup(wqkv[:, nq:nq + nkv]), dup(wqkv[:, nq + nkv:])],
                                     axis=1).astype(BF16)
            gain_row = jnp.concatenate([jnp.tile(attn_q_gain[j] * head_dim ** -0.5, n_heads),
                                        jnp.tile(attn_k_gain[j], 2 * n_kv)])[None]
            q, kd, vd = _qkv_project(xt, mod, norm_g[i, 0], w_qkvd, gain_row, head_block_ones, rope, d, plan)
            xt = _attention(q, kd, vd, attn_sinks[j], xt, mod, attn_w_o[j].astype(BF16), d, plan)
            xt = _moe_sublayer(xt, mod, norm_g[i, 1], moe_w_router[j], moe_b_router[j],
                               *moe_w16[j], d, plan)
    return xt.reshape(bsz, seq, dm)
```

```python
import functools
from typing import NamedTuple

import jax
import jax.numpy as jnp
from jax import lax
from jax.experimental import pallas as pl
from jax.experimental.pallas import tpu as pltpu

F32 = jnp.float32
BF16 = jnp.bfloat16
I32 = jnp.int32

V7X_LANES = 128
V7X_SUBLANES = 8
V7X_MXU_DIM = 256
ROW_ALIGN = 2 * V7X_SUBLANES
V7X_VMEM_LIMIT_BYTES = 56 * 1024 * 1024

ROPE_THETA = 500000.0
EPS = 1e-6
ATTN_BLOCK = 128
TOP_K = 2
MASK_VALUE = -1e30


class Dims(NamedTuple):
    batch: int
    seq: int
    d_model: int
    depth: int
    conv_kernel: int
    n_heads: int
    n_kv_heads: int
    head_dim: int
    d_ff_dense: int
    n_experts: int
    d_ff_expert: int


class Plan(NamedTuple):
    tm: int
    tm_attn: int
    tf_expert: int
    tg: int
    expert_sub: int
    proj_sub: int
    stage_sub: int
    tn_ada: int
    conv_rows: int
    conv_cols: int
    chunk: int


def _largest_tile(total, target, quantum):
    best = None
    t = quantum
    while t <= min(total, target):
        if total % t == 0:
            best = t
        t += quantum
    return best if best is not None else total


def _make_plan(d: Dims) -> Plan:
    tm = _largest_tile(d.seq, 512, ATTN_BLOCK)
    return Plan(
        tm=tm,
        tm_attn=_largest_tile(d.seq, 2 * tm, ATTN_BLOCK),
        tf_expert=_largest_tile(d.d_ff_expert, 1792, V7X_MXU_DIM),
        tg=_largest_tile(d.batch * d.seq, 1024, ATTN_BLOCK),
        expert_sub=2,
        proj_sub=4,
        stage_sub=1,
        tn_ada=_largest_tile(6 * d.d_model, 1536, V7X_LANES),
        conv_rows=128,
        conv_cols=V7X_LANES,
        chunk=ATTN_BLOCK // 2,
    )


def _params(*semantics):
    return pltpu.CompilerParams(dimension_semantics=semantics,
                                vmem_limit_bytes=V7X_VMEM_LIMIT_BYTES)


def _dot(a, b):
    return jnp.dot(a, b, preferred_element_type=F32)


def _sigmoid(x):
    return 1.0 / (1.0 + jnp.exp(-x))


def _norm_mod(x, g_row, shift_row, scale_row):
    y = x * lax.rsqrt(jnp.mean(x * x, axis=-1, keepdims=True) + EPS)
    return (y * g_row) * (1.0 + scale_row) + shift_row


def _full(shape):
    return pl.BlockSpec(shape, lambda *_: (0,) * len(shape))


def _ada_body(c_ref, w_ref, b_ref, o_ref):
    c = c_ref[...]
    ca = c * _sigmoid(c)
    o_ref[0] = jnp.dot(ca, w_ref[0], preferred_element_type=F32,
                       precision=lax.Precision.HIGHEST) + b_ref[0]


def _ada_all_layers(c, w_ada, b_ada, plan):
    depth, dm, n6 = w_ada.shape
    bsz = c.shape[0]
    tn = plan.tn_ada
    return pl.pallas_call(
        _ada_body,
        grid=(depth, n6 // tn),
        in_specs=[_full((bsz, dm)),
                  pl.BlockSpec((1, dm, tn), lambda i, j: (i, 0, j)),
                  pl.BlockSpec((1, 1, tn), lambda i, j: (i, 0, j))],
        out_specs=pl.BlockSpec((1, bsz, tn), lambda i, j: (i, 0, j)),
        out_shape=jax.ShapeDtypeStruct((depth, bsz, n6), F32),
        compiler_params=_params("arbitrary", "arbitrary"),
        name="adaln_mod",
    )(c, w_ada, b_ada.reshape(depth, 1, n6))


def _rope_body(pos_ref, f_ref, m1_ref, m2_ref, c_ref, s1_ref, s2_ref):
    ang = pos_ref[...].astype(F32) * f_ref[...]
    s = jnp.sin(ang)
    c_ref[...] = jnp.cos(ang)
    s1_ref[...] = s * m1_ref[...]
    s2_ref[...] = s * m2_ref[...]


def _rope_tables(positions, d: Dims, plan):
    t_total = d.batch * d.seq
    rot = d.head_dim // 4
    half = rot // 2
    inv_freq = ROPE_THETA ** (-jnp.arange(0, rot, 2, dtype=F32) / rot)
    lane = jnp.arange(V7X_LANES)
    in_head = lane % d.head_dim
    freq_row = jnp.where(in_head < rot, inv_freq[in_head % half], 0.0).astype(F32)[None]
    m1 = jnp.where((in_head >= half) & (in_head < rot), 1.0, 0.0).astype(F32)[None]
    m2 = jnp.where(in_head < half, -1.0, 0.0).astype(F32)[None]
    tm = plan.tm
    row = pl.BlockSpec((tm, V7X_LANES), lambda i: (i, 0))
    tab = jax.ShapeDtypeStruct((t_total, V7X_LANES), F32)
    return pl.pallas_call(
        _rope_body,
        grid=(t_total // tm,),
        in_specs=[pl.BlockSpec((tm, 1), lambda i: (i, 0)),
                  _full((1, V7X_LANES)), _full((1, V7X_LANES)), _full((1, V7X_LANES))],
        out_specs=[row, row, row],
        out_shape=[tab, tab, tab],
        compiler_params=_params("arbitrary"),
        name="rope_tables",
    )(positions.reshape(t_total, 1), freq_row, m1, m2)


def _cast_riders(refs, n_cast):
    for src, dst in zip(refs[:n_cast], refs[len(refs) - n_cast:]):
        dst[...] = src[...].astype(dst.dtype)


class Cast(NamedTuple):
    arrays: tuple
    layer: int
    n_layers: int


def _rider_specs(cast: Cast, n_steps):
    in_specs, out_specs, shapes = [], [], []
    for a in cast.arrays:
        layer_rows = a.shape[0] // cast.n_layers
        rows = layer_rows // n_steps
        assert rows * n_steps * cast.n_layers == a.shape[0] and rows % ROW_ALIGN == 0
        first = cast.layer * n_steps
        in_specs.append(pl.BlockSpec((rows, a.shape[1]), lambda i, first=first: (first + i, 0)))
        out_specs.append(pl.BlockSpec((rows, a.shape[1]), lambda i: (i, 0)))
        shapes.append(jax.ShapeDtypeStruct((layer_rows, a.shape[1]), BF16))
    return in_specs, out_specs, shapes


def _conv_body(x_ref, mod_ref, ng_ref, w1_ref, b1_ref, wd_ref, bd_ref, lg_ref, lb_ref,
               w2_ref, b2_ref, *rest, n_cast, dm, ktaps, halo, tiles_per_seq, rc, cb, n_sub):
    ubuf, wpk, vbuf = rest[-3:]
    o_ref = rest[n_cast]
    _cast_riders(rest[:-3], n_cast)
    m = pl.program_id(0)
    tm = x_ref.shape[0]
    mod = mod_ref[0]
    half = dm // 2
    high16 = jnp.uint32(0xFFFF0000)

    def bf16_bits(a):
        return lax.bitcast_convert_type(a.astype(BF16).astype(F32), jnp.uint32)

    def pack2(a):
        return bf16_bits(a[:, half:]) | (bf16_bits(a[:, :half]) >> 16)

    @pl.when(m % tiles_per_seq == 0)
    def _():
        ubuf[0:halo, :] = jnp.zeros((halo, half), ubuf.dtype)

    sub = tm // n_sub
    for sb in range(n_sub):
        rows = slice(sb * sub, (sb + 1) * sub)
        h = _norm_mod(x_ref[rows, :], ng_ref[...], mod[0:1], mod[1:2]).astype(BF16)
        u = _dot(h, w1_ref[...]) + b1_ref[...]
        ubuf[halo + sb * sub:halo + (sb + 1) * sub, :] = pack2(u[:, :dm] * _sigmoid(u[:, dm:]))
    wpk[...] = pack2(wd_ref[...])

    def chunk(ci, carry):
        r0 = pl.multiple_of(ci * rc, rc)
        lo, hi = [], []
        for cbi in range(half // cb):
            cols = slice(cbi * cb, (cbi + 1) * cb)
            win = pltpu.bitcast(ubuf[pl.ds(r0, rc + halo), cols], BF16)
            acc = None
            for r in range(V7X_SUBLANES):
                z = None
                for q in range(halo // V7X_SUBLANES):
                    dl = V7X_SUBLANES * q + r
                    if dl >= ktaps:
                        continue
                    s = halo - V7X_SUBLANES * (q + 1)
                    taps = jnp.broadcast_to(wpk[dl:dl + 1, cols], (rc + V7X_SUBLANES, cb))
                    term = pltpu.bitcast(taps, BF16) * win[2 * s:2 * (s + rc + V7X_SUBLANES), :]
                    z = term if z is None else z + term
                if z is None:
                    continue
                z = pltpu.bitcast(z, ubuf.dtype)
                if r:
                    z = pltpu.roll(z, r, 0)
                z = pltpu.bitcast(z[V7X_SUBLANES:V7X_SUBLANES + rc, :], BF16)
                acc = z if acc is None else acc + z
            acc = pltpu.bitcast(acc, ubuf.dtype)
            lo.append(lax.bitcast_convert_type(acc << 16, F32))
            hi.append(lax.bitcast_convert_type(acc & high16, F32))
        conv = jnp.concatenate(lo + hi, axis=1) + bd_ref[...]
        mu = jnp.mean(conv, axis=-1, keepdims=True)
        cen = conv - mu
        var = jnp.mean(cen * cen, axis=-1, keepdims=True)
        y = cen * lax.rsqrt(var + EPS) * lg_ref[...] + lb_ref[...]
        vbuf[pl.ds(r0, rc), :] = (y * _sigmoid(y)).astype(BF16)
        return carry

    lax.fori_loop(0, tm // rc, chunk, 0)
    ubuf[0:halo, :] = ubuf[tm:tm + halo, :]
    y2 = _dot(vbuf[...], w2_ref[...]) + b2_ref[...]
    o_ref[...] = x_ref[...] + mod[2:3] * y2


def _conv_sublayer(x, mod, ng, w1, b1, w_dw, b_dw, ln_g, ln_b, w2, b2, cast, d: Dims, plan):
    t_total, dm = x.shape
    tm = plan.tm
    n_steps = t_total // tm
    ktaps = w_dw.shape[0]
    halo = -(-ktaps // V7X_SUBLANES) * V7X_SUBLANES
    wd = jnp.zeros((halo, dm), F32).at[:ktaps].set(w_dw[::-1])
    tiles_per_seq = d.seq // tm
    row = pl.BlockSpec((tm, dm), lambda i: (i, 0))
    cast_in, cast_out, cast_shapes = _rider_specs(cast, n_steps)
    body = functools.partial(_conv_body, n_cast=len(cast.arrays), dm=dm, ktaps=ktaps, halo=halo,
                             tiles_per_seq=tiles_per_seq, rc=plan.conv_rows, cb=plan.conv_cols,
                             n_sub=1)
    return pl.pallas_call(
        body,
        grid=(n_steps,),
        in_specs=[row,
                  pl.BlockSpec((1, 6, dm), lambda i: (i // tiles_per_seq, 0, 0)),
                  _full((1, dm)), _full((dm, 2 * dm)), _full((1, 2 * dm)),
                  _full(wd.shape), _full((1, dm)), _full((1, dm)), _full((1, dm)),
                  _full((dm, dm)), _full((1, dm))] + cast_in,
        out_specs=[row] + cast_out,
        out_shape=[jax.ShapeDtypeStruct((t_total, dm), F32)] + cast_shapes,
        scratch_shapes=[pltpu.VMEM((halo + tm, dm // 2), jnp.uint32),
                        pltpu.VMEM((halo, dm // 2), jnp.uint32), pltpu.VMEM((tm, dm), BF16)],
        compiler_params=_params("arbitrary"),
        name="conv_sublayer",
    )(x, mod, ng[None], w1, b1[None], wd, b_dw[None], ln_g[None], ln_b[None], w2, b2[None],
      *cast.arrays)


def _swiglu(h, wg, wu, wd):
    g = _dot(h, wg)
    u = _dot(h, wu)
    return _dot(((g * _sigmoid(g)) * u).astype(BF16), wd)


def _ffn_body(x_ref, mod_ref, ng_ref, wg_ref, wu_ref, wd_ref, *rest, n_cast):
    o_ref = rest[n_cast]
    _cast_riders(rest, n_cast)
    mod = mod_ref[0]
    x = x_ref[...]
    h = _norm_mod(x, ng_ref[...], mod[3:4], mod[4:5]).astype(BF16)
    o_ref[...] = x + mod[5:6] * _swiglu(h, wg_ref[...], wu_ref[...], wd_ref[...])


def _ffn_sublayer(x, mod, ng, wg, wu, wd, cast, d: Dims, plan):
    t_total, dm = x.shape
    tm = plan.tm
    n_steps = t_total // tm
    tiles_per_seq = d.seq // tm
    row = pl.BlockSpec((tm, dm), lambda i: (i, 0))
    resident = pl.BlockSpec(memory_space=pltpu.VMEM)
    cast_in, cast_out, cast_shapes = _rider_specs(cast, n_steps)
    return pl.pallas_call(
        functools.partial(_ffn_body, n_cast=len(cast.arrays)),
        grid=(n_steps,),
        in_specs=[row,
                  pl.BlockSpec((1, 6, dm), lambda i: (i // tiles_per_seq, 0, 0)),
                  _full((1, dm)), resident, resident, resident] + cast_in,
        out_specs=[row] + cast_out,
        out_shape=[jax.ShapeDtypeStruct((t_total, dm), F32)] + cast_shapes,
        compiler_params=_params("arbitrary"),
        name="dense_swiglu",
    )(x, mod, ng[None], wg, wu, wd, *cast.arrays)


def _expert_body(te_ref, nv_ref, xs_ref, wg_ref, wu_ref, wd_ref, ys_ref, acc_ref, *, n_sub):
    i = pl.program_id(0)
    f = pl.program_id(1)
    rows = xs_ref.shape[0] // n_sub

    @pl.when(i < nv_ref[0])
    def _():
        @pl.when(f == 0)
        def _():
            acc_ref[...] = jnp.zeros_like(acc_ref)

        for s in range(n_sub):
            sl = slice(s * rows, (s + 1) * rows)
            acc_ref[sl, :] += _swiglu(xs_ref[sl, :], wg_ref[0], wu_ref[0], wd_ref[0])

        @pl.when(f == pl.num_programs(1) - 1)
        def _():
            ys_ref[...] = acc_ref[...].astype(ys_ref.dtype)

    @pl.when((i >= nv_ref[0]) & (f == 0))
    def _():
        ys_ref[...] = jnp.zeros_like(ys_ref)


def _expert_swiglu(xs, tile_expert, n_valid, n_tiles, wg, wu, wd, plan):
    dm = xs.shape[1]
    ff = wg.shape[2]
    tg, tf = plan.tg, plan.tf_expert
    p_rows = n_tiles * tg
    nf = ff // tf

    def row_map(i, f, te, nv):
        return (jnp.minimum(i, nv[0] - 1), 0)

    def fsel(i, f, nv):
        return jnp.where(i < nv[0], f, nf - 1)

    grid_spec = pltpu.PrefetchScalarGridSpec(
        num_scalar_prefetch=2,
        grid=(p_rows // tg, nf),
        in_specs=[pl.BlockSpec((tg, dm), row_map),
                  pl.BlockSpec((1, dm, tf), lambda i, f, te, nv: (te[i], 0, fsel(i, f, nv))),
                  pl.BlockSpec((1, dm, tf), lambda i, f, te, nv: (te[i], 0, fsel(i, f, nv))),
                  pl.BlockSpec((1, tf, dm), lambda i, f, te, nv: (te[i], fsel(i, f, nv), 0))],
        out_specs=pl.BlockSpec((tg, dm), lambda i, f, te, nv: (i, 0)),
        scratch_shapes=[pltpu.VMEM((tg, dm), F32)],
    )
    return pl.pallas_call(
        functools.partial(_expert_body, n_sub=plan.expert_sub),
        grid_spec=grid_spec,
        out_shape=jax.ShapeDtypeStruct((p_rows, dm), BF16),
        compiler_params=_params("arbitrary", "arbitrary"),
        name="expert_swiglu",
    )(tile_expert, n_valid, xs, wg, wu, wd)


def _qkv_body(x_ref, mod_ref, ng_ref, w_ref, gain_ref, bd_ref, c_ref, s1_ref, s2_ref,
              q_ref, k_ref, v_ref, *, nq, nk, head_dim, half_rot, n_sub):
    mod = mod_ref[0]
    bd = bd_ref[...]
    sub = x_ref.shape[0] // n_sub
    for sb in range(n_sub):
        rows = slice(sb * sub, (sb + 1) * sub)
        h = _norm_mod(x_ref[rows, :], ng_ref[...], mod[0:1], mod[1:2]).astype(BF16)
        qkv = _dot(h, w_ref[...])
        v_ref[rows, :] = qkv[:, nq + nk:].astype(BF16)
        cos = c_ref[rows, :]
        sin_hi = s1_ref[rows, :]
        sin_lo = s2_ref[rows, :]
        for j in range((nq + nk) // V7X_MXU_DIM):
            blk = qkv[:, j * V7X_MXU_DIM:(j + 1) * V7X_MXU_DIM]
            ss = _dot((blk * blk).astype(BF16), bd)
            yn = ((blk * lax.rsqrt(ss * (1.0 / head_dim) + EPS))
                  * gain_ref[:, j * V7X_MXU_DIM:(j + 1) * V7X_MXU_DIM])
            for p in range(V7X_MXU_DIM // V7X_LANES):
                xx = yn[:, p * V7X_LANES:(p + 1) * V7X_LANES]
                rot = (xx * cos + pltpu.roll(xx, half_rot, 1) * sin_hi
                       + pltpu.roll(xx, V7X_LANES - half_rot, 1) * sin_lo).astype(BF16)
                col = j * V7X_MXU_DIM + p * V7X_LANES
                if col < nq:
                    q_ref[rows, col:col + V7X_LANES] = rot
                else:
                    k_ref[rows, col - nq:col - nq + V7X_LANES] = rot


def _qkv_project(x, mod, ng, w_qkvd, gain_row, bd, rope, d: Dims, plan):
    t_total, dm = x.shape
    tm = plan.tm_attn
    nq = d.n_heads * d.head_dim
    nk = d.n_kv_heads * 2 * d.head_dim
    tiles_per_seq = d.seq // tm
    row = pl.BlockSpec((tm, dm), lambda i: (i, 0))
    tab = pl.BlockSpec((tm, V7X_LANES), lambda i: (i, 0))
    body = functools.partial(_qkv_body, nq=nq, nk=nk, head_dim=d.head_dim,
                             half_rot=d.head_dim // 8, n_sub=plan.proj_sub * (tm // plan.tm))
    return pl.pallas_call(
        body,
        grid=(t_total // tm,),
        in_specs=[row,
                  pl.BlockSpec((1, 6, dm), lambda i: (i // tiles_per_seq, 0, 0)),
                  _full((1, dm)), _full((dm, nq + 2 * nk)), _full((1, nq + nk)),
                  _full((V7X_MXU_DIM, V7X_MXU_DIM)), tab, tab, tab],
        out_specs=[pl.BlockSpec((tm, nq), lambda i: (i, 0)),
                   pl.BlockSpec((tm, nk), lambda i: (i, 0)),
                   pl.BlockSpec((tm, nk), lambda i: (i, 0))],
        out_shape=[jax.ShapeDtypeStruct((t_total, nq), BF16),
                   jax.ShapeDtypeStruct((t_total, nk), BF16),
                   jax.ShapeDtypeStruct((t_total, nk), BF16)],
        compiler_params=_params("arbitrary"),
        name="qkv_project",
    )(x, mod, ng[None], w_qkvd, gain_row, bd, *rope)


def _attn_body(q_ref, k_ref, v_ref, kp_ref, vp_ref, sink_ref, x_ref, mod_ref, wo_ref, o_ref,
               o_scr, *, n_kv, group, head_dim, tiles_per_seq):
    m = pl.program_id(0)
    tq = q_ref.shape[0]
    blk = ATTN_BLOCK
    qi = lax.broadcasted_iota(I32, (blk, blk), 0)
    kj = lax.broadcasted_iota(I32, (blk, blk), 1)
    from_prev = kj > qi
    kmin = jnp.where(m % tiles_per_seq == 0, blk, 0)
    lo = lax.broadcasted_iota(I32, (blk, V7X_LANES), 1) < head_dim
    zero = jnp.zeros((blk, V7X_LANES), BF16)
    pairs = group // 2
    for n in range(tq // blk):
        rows = slice(n * blk, (n + 1) * blk)
        if n == 0:
            prev_ok = from_prev & (kj >= kmin)
            kprev, vprev = kp_ref[...], vp_ref[...]
        else:
            prev_ok = None
            prev = slice((n - 1) * blk, n * blk)
            kprev, vprev = k_ref[prev, :], v_ref[prev, :]
        kcur, vcur = k_ref[rows, :], v_ref[rows, :]
        for g in range(n_kv):
            gl = slice(g * V7X_LANES, (g + 1) * V7X_LANES)
            kk = jnp.concatenate([kprev[:, gl], kcur[:, gl]], axis=0)
            vv = jnp.concatenate([vprev[:, gl], vcur[:, gl]], axis=0)
            slabs = []
            for pr in range(pairs):
                c0 = (g * pairs + pr) * V7X_LANES
                qp = q_ref[rows, c0:c0 + V7X_LANES]
                slabs += [jnp.where(lo, qp, zero), jnp.where(lo, zero, qp)]
            qs = jnp.concatenate(slabs, axis=0)
            s = lax.dot_general(qs, kk, (((1,), (1,)), ((), ())), preferred_element_type=F32)
            ps, dens = [], []
            for hh in range(group):
                s_prev = s[hh * blk:(hh + 1) * blk, :blk]
                s_cur = s[hh * blk:(hh + 1) * blk, blk:]
                if prev_ok is None:
                    sh = jnp.where(from_prev, s_prev, s_cur)
                else:
                    sh = jnp.where(prev_ok, s_prev, jnp.where(from_prev, MASK_VALUE, s_cur))
                sink = sink_ref[0, g * group + hh]
                mx = jnp.maximum(jnp.max(sh, axis=-1, keepdims=True), sink)
                p = jnp.exp(sh - mx)
                dens.append(jnp.sum(p, axis=-1, keepdims=True) + jnp.exp(sink - mx))
                ps.append(jnp.concatenate([jnp.where(from_prev, p, 0.0), jnp.where(from_prev, 0.0, p)],
                                          axis=1).astype(BF16))
            o = _dot(jnp.concatenate(ps, axis=0), vv)
            for pr in range(pairs):
                h0, h1 = 2 * pr, 2 * pr + 1
                oa = o[h0 * blk:(h0 + 1) * blk, :] / dens[h0]
                ob = o[h1 * blk:(h1 + 1) * blk, :] / dens[h1]
                c0 = (g * pairs + pr) * V7X_LANES
                o_scr[rows, c0:c0 + V7X_LANES] = jnp.where(lo, oa, ob).astype(BF16)
    mod = mod_ref[0]
    o_ref[...] = x_ref[...] + mod[2:3] * _dot(o_scr[...], wo_ref[...])


def _attention(q, kd, vd, sinks, x, mod, wo, d: Dims, plan):
    t_total, dm = x.shape
    tq = plan.tm_attn
    nq = q.shape[1]
    nk = kd.shape[1]
    tiles_per_seq = d.seq // tq
    bpt = tq // ATTN_BLOCK
    group = d.n_heads // d.n_kv_heads
    prev = pl.BlockSpec((ATTN_BLOCK, nk), lambda i: (jnp.maximum(i * bpt - 1, 0), 0))
    cur = pl.BlockSpec((tq, nk), lambda i: (i, 0))
    body = functools.partial(_attn_body, n_kv=d.n_kv_heads, group=group, head_dim=d.head_dim,
                             tiles_per_seq=tiles_per_seq)
    return pl.pallas_call(
        body,
        grid=(t_total // tq,),
        in_specs=[pl.BlockSpec((tq, nq), lambda i: (i, 0)), cur, cur, prev, prev,
                  pl.BlockSpec(memory_space=pltpu.SMEM),
                  pl.BlockSpec((tq, dm), lambda i: (i, 0)),
                  pl.BlockSpec((1, 6, dm), lambda i: (i // tiles_per_seq, 0, 0)),
                  _full((nq, dm))],
        out_specs=pl.BlockSpec((tq, dm), lambda i: (i, 0)),
        out_shape=jax.ShapeDtypeStruct((t_total, dm), F32),
        scratch_shapes=[pltpu.VMEM((tq, nq), BF16)],
        compiler_params=_params("arbitrary"),
        name="swa_attention",
    )(q, kd, vd, kd, vd, sinks[None], x, mod, wo)


def _router_body(x_ref, mod_ref, ng_ref, wr_ref, br_ref, ltri_ref, h_ref, meta_ref, cnt_ref,
                 *, n_experts):
    tt = x_ref.shape[0]
    mod = mod_ref[0]
    h2 = _norm_mod(x_ref[...], ng_ref[...], mod[3:4], mod[4:5])
    h_hi = h2.astype(BF16)
    h_ref[...] = h_hi
    h_lo = (h2 - h_hi.astype(F32)).astype(BF16)
    both = _dot(h_hi, wr_ref[...])
    logits = (both[:, :V7X_LANES] + both[:, V7X_LANES:]
              + _dot(h_lo, wr_ref[:, :V7X_LANES])) + br_ref[...]
    lane = lax.broadcasted_iota(I32, (tt, V7X_LANES), 1).astype(F32)
    neg = jnp.float32(-jnp.inf)
    logits = jnp.where(lane < n_experts, logits, neg)
    big = jnp.float32(V7X_LANES)
    m1 = jnp.max(logits, axis=-1, keepdims=True)
    i1 = jnp.min(jnp.where(logits == m1, lane, big), axis=-1, keepdims=True)
    rest = jnp.where(lane == i1, neg, logits)
    m2 = jnp.max(rest, axis=-1, keepdims=True)
    i2 = jnp.min(jnp.where(rest == m2, lane, big), axis=-1, keepdims=True)
    e2 = jnp.exp(m2 - m1)
    w1 = 1.0 / (1.0 + e2)
    w2 = e2 / (1.0 + e2)

    sel1 = lane == i1
    sel2 = lane == i2
    assigned = jnp.where(sel1 | sel2, 1.0, 0.0)
    before = _dot(ltri_ref[...], assigned.astype(BF16))
    r1 = jnp.sum(jnp.where(sel1, before, 0.0), axis=-1, keepdims=True)
    r2 = jnp.sum(jnp.where(sel2, before, 0.0), axis=-1, keepdims=True)
    cnt_ref[0] = jnp.broadcast_to(jnp.sum(assigned, axis=0, keepdims=True), cnt_ref.shape[1:])
    meta = jnp.zeros((tt, V7X_LANES), F32)
    for col, val in enumerate((i1, i2, r1, r2, w1, w2)):
        meta = jnp.where(lane == col, val, meta)
    meta_ref[...] = meta


META_EXPERT, META_RANK, META_GATE = 0, 2, 4


def _route(x, mod, ng, w_router, b_router, d: Dims, plan):
    t_total, dm = x.shape
    tt = plan.tm
    ne = d.n_experts
    n_rt = t_total // tt
    tiles_per_seq = d.seq // tt
    wr = jnp.zeros((dm, V7X_LANES), F32).at[:, :ne].set(w_router)
    wr_hi = wr.astype(BF16)
    wr = jnp.concatenate([wr_hi, (wr - wr_hi.astype(F32)).astype(BF16)], axis=1)
    br = jnp.zeros((1, V7X_LANES), F32).at[0, :ne].set(b_router)
    ltri = jnp.tril(jnp.ones((tt, tt), BF16), -1)
    row = pl.BlockSpec((tt, dm), lambda i: (i, 0))
    meta = pl.BlockSpec((tt, V7X_LANES), lambda i: (i, 0))
    return pl.pallas_call(
        functools.partial(_router_body, n_experts=ne),
        grid=(n_rt,),
        in_specs=[row,
                  pl.BlockSpec((1, 6, dm), lambda i: (i // tiles_per_seq, 0, 0)),
                  _full((1, dm)), _full((dm, 2 * V7X_LANES)), _full((1, V7X_LANES)), _full((tt, tt))],
        out_specs=[row, meta, pl.BlockSpec((1, V7X_SUBLANES, V7X_LANES), lambda i: (i, 0, 0))],
        out_shape=[jax.ShapeDtypeStruct((t_total, dm), BF16),
                   jax.ShapeDtypeStruct((t_total, V7X_LANES), F32),
                   jax.ShapeDtypeStruct((n_rt, V7X_SUBLANES, V7X_LANES), F32)],
        compiler_params=_params("arbitrary"),
        name="moe_router",
    )(x, mod, ng[None], wr, br, ltri)


def _chunks(count, ch):
    return lax.shift_right_logical(count + (ch - 1), jnp.int32(ch.bit_length() - 1))


def _stage_offsets(cnt_ref, base, n_experts, ch):
    offs = []
    total = jnp.int32(0)
    for e in range(n_experts):
        offs.append(total)
        total = total + _chunks(cnt_ref[base + e], ch) * ch
    return offs, total


def _stage_pos(meta, lanes, offs):
    off_row = jnp.zeros((1, lanes.shape[1]), F32)
    for e, off in enumerate(offs):
        off_row = jnp.where(lanes[0:1, :] == e, off.astype(F32), off_row)
    pos = []
    for k in range(TOP_K):
        expert = meta[:, META_EXPERT + k:META_EXPERT + k + 1]
        seg_off = jnp.sum(jnp.where(lanes == expert, off_row, 0.0), axis=-1, keepdims=True)
        pos.append(seg_off + meta[:, META_RANK + k:META_RANK + k + 1])
    return pos


def _dispatch_body(seg_ref, cnt_ref, zf_ref, nv_ref, h_ref, meta_ref, xs_hbm, stage, sem,
                   *, n_experts, ch, tg, n_fill_tiles, n_sub):
    m = pl.program_id(0)
    tt = h_ref.shape[0]
    buf = m % 2

    def block_copy(b, src_row, dst_row, rows):
        return pltpu.make_async_copy(stage.at[b, pl.ds(src_row, rows), :],
                                     xs_hbm.at[pl.ds(dst_row, rows), :], sem.at[b])

    def drain(b, n_chunks):
        def one(i, c):
            block_copy(b, 0, 0, ch).wait()
            return c

        lax.fori_loop(0, n_chunks, one, 0)

    @pl.when(m == 0)
    def _():
        stage[0, 0:tg + ch, :] = jnp.zeros((tg + ch, stage.shape[2]), stage.dtype)
        for e in range(n_experts):
            cp = block_copy(0, 0, pl.multiple_of(zf_ref[e], ROW_ALIGN), tg + ch)
            cp.start()
            cp.wait()

        def fill(j, c):
            cp = block_copy(0, 0, pl.multiple_of(j * tg, tg), tg)
            cp.start()
            cp.wait()
            return c

        lax.fori_loop(nv_ref[0], n_fill_tiles, fill, 0)

    offs, total = _stage_offsets(cnt_ref, m * n_experts, n_experts, ch)
    lanes = lax.broadcasted_iota(I32, (tt, V7X_LANES), 1).astype(F32)
    pos = _stage_pos(meta_ref[...], lanes, offs)
    cols = jnp.full((tt, V7X_LANES), -1.0, F32)
    for k in range(TOP_K):
        cols = jnp.where(lanes == k, pos[k], cols)
    pos_rows = jnp.transpose(cols)
    sub = stage.shape[1] // n_sub
    for sb in range(n_sub):
        slot = (lax.broadcasted_iota(I32, (sub, tt), 0) + sb * sub).astype(F32)
        onehot = jnp.zeros(slot.shape, F32)
        for k in range(TOP_K):
            onehot = jnp.where(slot == pos_rows[k:k + 1, :], 1.0, onehot)
        stage[buf, sb * sub:(sb + 1) * sub, :] = _dot(onehot.astype(BF16), h_ref[...]).astype(stage.dtype)

    @pl.when(m > 0)
    def _():
        _, prev_total = _stage_offsets(cnt_ref, jnp.maximum(m - 1, 0) * n_experts, n_experts, ch)
        drain(1 - buf, _chunks(prev_total, ch))

    for e in range(n_experts):
        start = seg_ref[m * n_experts + e]

        def send(c, carry, start=start, off=offs[e]):
            block_copy(buf, pl.multiple_of(off + c * ch, ch),
                       pl.multiple_of(start + c * ch, ROW_ALIGN), ch).start()
            return carry

        lax.fori_loop(0, _chunks(cnt_ref[m * n_experts + e], ch), send, 0)

    @pl.when(m == pl.num_programs(0) - 1)
    def _():
        drain(buf, _chunks(total, ch))


def _dispatch(h2, meta, seg_start, seg_cnt, zfill_start, n_valid, n_rows, d: Dims, plan):
    t_total, dm = h2.shape
    tt, tg, ch = plan.tm, plan.tg, plan.chunk
    ne = d.n_experts
    grid_spec = pltpu.PrefetchScalarGridSpec(
        num_scalar_prefetch=4,
        grid=(t_total // tt,),
        in_specs=[pl.BlockSpec((tt, dm), lambda i, *_: (i, 0)),
                  pl.BlockSpec((tt, V7X_LANES), lambda i, *_: (i, 0))],
        out_specs=pl.BlockSpec(memory_space=pl.ANY),
        scratch_shapes=[pltpu.VMEM((2, max(TOP_K * tt + ne * ch, tg + ch), dm), BF16),
                        pltpu.SemaphoreType.DMA((2,))],
    )
    body = functools.partial(_dispatch_body, n_experts=ne, ch=ch, tg=tg, n_fill_tiles=n_rows // tg,
                             n_sub=plan.stage_sub)
    return pl.pallas_call(
        body,
        grid_spec=grid_spec,
        out_shape=jax.ShapeDtypeStruct((n_rows, dm), BF16),
        compiler_params=_params("arbitrary"),
        name="moe_dispatch",
    )(seg_start, seg_cnt, zfill_start, n_valid, h2, meta)


def _combine_body(seg_ref, cnt_ref, ys_hbm, x_ref, mod_ref, meta_ref, o_ref, stage, sem,
                  *, n_experts, ch, n_sub):
    m = pl.program_id(0)
    tt = x_ref.shape[0]
    buf = m % 2

    def fetch(b, src_row, dst_row):
        return pltpu.make_async_copy(ys_hbm.at[pl.ds(src_row, ch), :],
                                     stage.at[b, pl.ds(dst_row, ch), :], sem.at[b])

    def fetch_tile(tile, b):
        offs, _ = _stage_offsets(cnt_ref, tile * n_experts, n_experts, ch)
        for e in range(n_experts):
            start = seg_ref[tile * n_experts + e]

            def issue(c, carry, start=start, off=offs[e]):
                fetch(b, pl.multiple_of(start + c * ch, ROW_ALIGN),
                      pl.multiple_of(off + c * ch, ch)).start()
                return carry

            lax.fori_loop(0, _chunks(cnt_ref[tile * n_experts + e], ch), issue, 0)

    @pl.when(m == 0)
    def _():
        stage[...] = jnp.zeros_like(stage)
        fetch_tile(m, buf)

    @pl.when(m + 1 < pl.num_programs(0))
    def _():
        fetch_tile(m + 1, 1 - buf)

    offs, total = _stage_offsets(cnt_ref, m * n_experts, n_experts, ch)
    meta = meta_ref[...]
    lanes = lax.broadcasted_iota(I32, (tt, V7X_LANES), 1).astype(F32)
    pos = _stage_pos(meta, lanes, offs)
    def drain(i, c):
        fetch(buf, 0, 0).wait()
        return c

    lax.fori_loop(0, _chunks(total, ch), drain, 0)
    sub = tt // n_sub
    slot = lax.broadcasted_iota(I32, (sub, stage.shape[1]), 1).astype(F32)
    for sb in range(n_sub):
        rows = slice(sb * sub, (sb + 1) * sub)
        weights = jnp.zeros(slot.shape, F32)
        for k in range(TOP_K):
            weights = jnp.where(slot == pos[k][rows, :], meta[rows, META_GATE + k:META_GATE + k + 1], weights)
        y = _dot(weights.astype(BF16), stage[buf])
        o_ref[rows, :] = x_ref[rows, :] + mod_ref[0][5:6] * y


def _combine(ys, seg_start, seg_cnt, x, mod, meta, d: Dims, plan):
    t_total, dm = x.shape
    tt, ch = plan.tm, plan.chunk
    ne = d.n_experts
    tiles_per_seq = d.seq // tt
    row = pl.BlockSpec((tt, dm), lambda i, *_: (i, 0))
    grid_spec = pltpu.PrefetchScalarGridSpec(
        num_scalar_prefetch=2,
        grid=(t_total // tt,),
        in_specs=[pl.BlockSpec(memory_space=pl.ANY), row,
                  pl.BlockSpec((1, 6, dm), lambda i, *_: (i // tiles_per_seq, 0, 0)),
                  pl.BlockSpec((tt, V7X_LANES), lambda i, *_: (i, 0))],
        out_specs=row,
        scratch_shapes=[pltpu.VMEM((2, TOP_K * tt + ne * ch, dm), BF16),
                        pltpu.SemaphoreType.DMA((2,))],
    )
    return pl.pallas_call(
        functools.partial(_combine_body, n_experts=ne, ch=ch, n_sub=plan.proj_sub),
        grid_spec=grid_spec,
        out_shape=jax.ShapeDtypeStruct((t_total, dm), F32),
        compiler_params=_params("arbitrary"),
        name="moe_combine",
    )(seg_start, seg_cnt, ys, x, mod, meta)


def _round_up(v, q):
    return (v + q - 1) // q * q


def _moe_sublayer(x, mod, ng, w_router, b_router, wg, wu, wd, d: Dims, plan):
    t_total, dm = x.shape
    ne, tg, ch, tt = d.n_experts, plan.tg, plan.chunk, plan.tm
    n_rt = t_total // tt
    h2, meta, cnt = _route(x, mod, ng, w_router, b_router, d, plan)
    seg_cnt = cnt[:, 0, :ne].astype(I32)
    seg_len = _round_up(seg_cnt, ROW_ALIGN)
    used = jnp.sum(seg_len, axis=0)
    region = _round_up(used + ch, tg)
    region_end = jnp.cumsum(region)
    region_start = region_end - region
    seg_start = region_start[None, :] + jnp.cumsum(seg_len, axis=0) - seg_len
    n_valid = (region_end[ne - 1] // tg).astype(I32)
    n_tiles = -(-(TOP_K * t_total + n_rt * ne * (ROW_ALIGN - 1) + ne * ch) // tg) + ne
    tile_id = jnp.minimum(jnp.arange(n_tiles, dtype=I32), n_valid - 1)
    tile_expert = jnp.sum(tile_id[:, None] >= (region_end // tg)[None, :ne - 1], axis=1).astype(I32)
    seg_start = seg_start.reshape(-1).astype(I32)
    seg_cnt = seg_cnt.reshape(-1)
    xs = _dispatch(h2, meta, seg_start, seg_cnt, (region_start + used).astype(I32), n_valid.reshape(1),
                   (n_tiles + 2) * tg, d, plan)
    ys = _expert_swiglu(xs, tile_expert, n_valid.reshape(1), n_tiles, wg, wu, wd, plan)
    return _combine(ys, seg_start, seg_cnt, x, mod, meta, d, plan)


def kernel(x, c, positions, norm_g, w_ada, b_ada, conv_w_pw1, conv_b_pw1, conv_w_dw, conv_b_dw, conv_ln_g, conv_ln_b, conv_w_pw2, conv_b_pw2, attn_w_qkv, attn_q_gain, attn_k_gain, attn_sinks, attn_w_o, ffn_w_gate, ffn_w_up, ffn_w_down, moe_w_router, moe_b_router, moe_w_gate, moe_w_up, moe_w_down):
    bsz, seq, dm = x.shape
    head_dim = attn_q_gain.shape[1]
    n_heads = attn_sinks.shape[1]
    n_kv = (attn_w_qkv.shape[2] // head_dim - n_heads) // 2
    d = Dims(batch=bsz, seq=seq, d_model=dm, depth=norm_g.shape[0], conv_kernel=conv_w_dw.shape[1],
             n_heads=n_heads, n_kv_heads=n_kv, head_dim=head_dim, d_ff_dense=ffn_w_gate.shape[2],
             n_experts=moe_w_router.shape[2], d_ff_expert=moe_w_gate.shape[3])
    assert 2 * head_dim == V7X_LANES and (n_heads // n_kv) % 2 == 0 and n_heads % n_kv == 0
    assert seq % ATTN_BLOCK == 0 and dm % V7X_MXU_DIM == 0
    assert (n_heads + 2 * n_kv) * head_dim % V7X_MXU_DIM == 0 and n_kv * 2 * head_dim % V7X_MXU_DIM == 0
    plan = _make_plan(d)
    t_total = bsz * seq
    nq = n_heads * head_dim

    mod_all = _ada_all_layers(c, w_ada, b_ada, plan).reshape(d.depth, bsz, 6, dm)
    rope = _rope_tables(positions, d, plan)
    blk_id = jnp.arange(V7X_MXU_DIM) // head_dim
    head_block_ones = (blk_id[:, None] == blk_id[None, :]).astype(BF16)
    n_moe = moe_w_gate.shape[0]
    ne, ffe = d.n_experts, d.d_ff_expert
    flat = lambda w: w.reshape(-1, w.shape[-1])
    moe_w16 = {}

    xt = x.reshape(t_total, dm)
    for i in range(d.depth):
        j = i // 2
        mod = mod_all[i]
        if i % 2 == 0:
            ride = j < n_moe
            gate_up = Cast((flat(moe_w_gate), flat(moe_w_up)) if ride else (), j, n_moe)
            down = Cast((flat(moe_w_down),) if ride else (), j, n_moe)
            xt, *gu16 = _conv_sublayer(xt, mod, norm_g[i, 0], conv_w_pw1[j].astype(BF16), conv_b_pw1[j],
                                       conv_w_dw[j], conv_b_dw[j], conv_ln_g[j], conv_ln_b[j],
                                       conv_w_pw2[j].astype(BF16), conv_b_pw2[j], gate_up, d, plan)
            xt, *dn16 = _ffn_sublayer(xt, mod, norm_g[i, 1], ffn_w_gate[j].astype(BF16),
                                      ffn_w_up[j].astype(BF16), ffn_w_down[j].astype(BF16), down, d, plan)
            if ride:
                moe_w16[j] = (gu16[0].reshape(ne, dm, ffe), gu16[1].reshape(ne, dm, ffe),
                              dn16[0].reshape(ne, ffe, dm))
        else:
            wqkv = attn_w_qkv[j]
            nkv = n_kv * head_dim
            dup = lambda w: jnp.repeat(w.reshape(dm, n_kv, 1, head_dim), 2, axis=2).reshape(dm, 2 * nkv)
            w_qkvd = jnp.concatenate([wqkv[:, :nq], dup(wqkv[:, nq:nq + nkv]), dup(wqkv[:, nq + nkv:])],
                                     axis=1).astype(BF16)
            gain_row = jnp.concatenate([jnp.tile(attn_q_gain[j] * head_dim ** -0.5, n_heads),
                                        jnp.tile(attn_k_gain[j], 2 * n_kv)])[None]
            q, kd, vd = _qkv_project(xt, mod, norm_g[i, 0], w_qkvd, gain_row, head_block_ones, rope, d, plan)
            xt = _attention(q, kd, vd, attn_sinks[j], xt, mod, attn_w_o[j].astype(BF16), d, plan)
            xt = _moe_sublayer(xt, mod, norm_g[i, 1], moe_w_router[j], moe_b_router[j],
                               *moe_w16[j], d, plan)
    return xt.reshape(bsz, seq, dm)
```
